```python
import math
import jax, jax.numpy as jnp
from jax import lax
import numpy as np

D_MODEL = 1024
BATCH = 8
SEQ = 8192
DEPTH = 4

HEAD_DIM = 64
N_HEADS = D_MODEL // HEAD_DIM
HEADS_A = 3 * N_HEADS // 8
HEADS_B = N_HEADS // 4
KV_HEADS_B = HEADS_B // 2
HEADS_C = N_HEADS - HEADS_A - HEADS_B
DILATED_PAIRS = ((128, 1), (512, 4), (2048, 16))
WINDOW_B = 128
MLSTM_CHUNK = 64
CONV_WIDTH = 5
FORGET_BIAS_LO = 3.0
FORGET_BIAS_HI = 6.0
N_BUCKETS = 32
REL_MAX_DIST = 1024
D_FF = -(-8 * D_MODEL // (3 * 256)) * 256
PLE_DIM = 256
EPS = 1e-6
NEG = -1e30
COL_SIZES = (HEADS_A * HEAD_DIM,) * 3 + (HEADS_B * HEAD_DIM, KV_HEADS_B * HEAD_DIM, KV_HEADS_B * HEAD_DIM) + (HEADS_C * HEAD_DIM,) * 4 + (4 * HEADS_C,)
D_IN = sum(COL_SIZES)

kernel_name = "hybrid_dilated_swa_mlstm_encoder"


def rmsnorm(x, g):
    xf = x.astype(jnp.float32)
    y = xf * lax.rsqrt(jnp.mean(xf * xf, axis=-1, keepdims=True) + EPS)
    return (y * g.astype(jnp.float32)).astype(x.dtype)


def t5_bucket(rel):
    half_b = N_BUCKETS // 2
    max_exact = half_b // 2
    n = jnp.abs(rel)
    nf = jnp.maximum(n, 1).astype(jnp.float32)
    large = max_exact + (jnp.log(nf / max_exact) / math.log(REL_MAX_DIST / max_exact) * (half_b - max_exact)).astype(jnp.int32)
    large = jnp.minimum(large, half_b - 1)
    return jnp.where(rel > 0, half_b, 0) + jnp.where(n < max_exact, n, large)


def banded_attention(q, k, v, bias_table, half, blk, dil, sink=None):
    N, L, H, dh = q.shape
    G = k.shape[2]
    R = H // G
    nb = -(-L // blk)
    Lp = nb * blk
    pad = Lp - L
    qp = jnp.pad(q, ((0, 0), (0, pad), (0, 0), (0, 0))).reshape(N, nb, blk, G, R, dh)

    def kblocks(t):
        tp = jnp.pad(t, ((0, 0), (blk, pad + blk), (0, 0), (0, 0))).reshape(N, nb + 2, blk, G, dh)
        return jnp.concatenate([tp[:, :-2], tp[:, 1:-1], tp[:, 2:]], axis=2)

    kb, vb = kblocks(k), kblocks(v)
    logits = jnp.einsum('nbqgrd,nbkgd->nbgrqk', qp, kb).astype(jnp.float32) * (dh ** -0.5)
    iq = jnp.arange(blk, dtype=jnp.int32)
    ik = jnp.arange(3 * blk, dtype=jnp.int32)
    rel = ik[None, :] - blk - iq[:, None]
    kpos = jnp.arange(nb, dtype=jnp.int32)[:, None] * blk - blk + ik[None, :]
    mask = (jnp.abs(rel) <= half)[None] & ((kpos >= 0) & (kpos < L))[:, None, :]
    bias = jnp.transpose(bias_table[t5_bucket(rel * dil)], (2, 0, 1)).reshape(G, R, blk, 3 * blk)
    logits = jnp.where(mask[None, :, None, None], logits + bias.astype(jnp.float32), NEG)
    m = jnp.max(logits, axis=-1)
    if sink is not None:
        s = sink.astype(jnp.float32).reshape(G, R)[None, None, :, :, None]
        m = jnp.maximum(m, s)
    pexp = jnp.exp(logits - m[..., None])
    denom = jnp.sum(pexp, axis=-1)
    if sink is not None:
        denom = denom + jnp.exp(s - m)
    out = jnp.einsum('nbgrqk,nbkgd->nbqgrd', pexp, vb.astype(jnp.float32))
    out = out / jnp.moveaxis(denom, -1, 2)[..., None]
    lse = jnp.moveaxis(m + jnp.log(denom), -1, 2)
    return out.reshape(N, Lp, H, dh)[:, :L], lse.reshape(N, Lp, H)[:, :L]


def dilated_attention(q, k, v, bias_table):
    B, S, H, dh = q.shape
    outs, lses = [], []
    for (w, d) in DILATED_PAIRS:
        half = w // (2 * d)

        def to_sub(t):
            return t.reshape(B, S // d, d, H, dh).transpose(0, 2, 1, 3, 4).reshape(B * d, S // d, H, dh)

        o, l = banded_attention(to_sub(q), to_sub(k), to_sub(v), bias_table, half, half, d)
        outs.append(o.reshape(B, d, S // d, H, dh).transpose(0, 2, 1, 3, 4).reshape(B, S, H, dh))
        lses.append(l.reshape(B, d, S // d, H).transpose(0, 2, 1, 3).reshape(B, S, H))
    wts = jax.nn.softmax(jnp.stack(lses, 0), axis=0)
    return jnp.einsum('gbsh,gbshd->bshd', wts, jnp.stack(outs, 0))


def mlstm_direction(q, k, v, ig, fg):
    N, H, S, dh = q.shape
    L = MLSTM_CHUNK
    nc = S // L
    q = q.reshape(N, H, nc, L, dh)
    k = k.reshape(N, H, nc, L, dh)
    v = v.reshape(N, H, nc, L, dh)
    ig = ig.reshape(N, H, nc, L)
    b = jnp.cumsum(jax.nn.log_sigmoid(fg).reshape(N, H, nc, L), axis=-1)
    b_last = b[..., -1]
    w = b_last[..., None] - b + ig
    m_loc = jnp.max(w, axis=-1)
    e = jnp.exp(w - m_loc[..., None])
    C_loc = jnp.einsum('nhcs,nhcsd,nhcse->nhcde', e, k, v)
    n_loc = jnp.einsum('nhcs,nhcsd->nhcd', e, k)

    def step(carry, xs):
        C, n, m = carry
        bl, Cl, nl, ml = xs
        m_new = jnp.maximum(bl + m, ml)
        a = jnp.exp(bl + m - m_new)
        c = jnp.exp(ml - m_new)
        C_new = a[..., None, None] * C + c[..., None, None] * Cl
        n_new = a[..., None] * n + c[..., None] * nl
        return (C_new, n_new, m_new), (C, n, m)

    init = (jnp.zeros((N, H, dh, dh), jnp.float32), jnp.zeros((N, H, dh), jnp.float32), jnp.zeros((N, H), jnp.float32))
    xs = tuple(jnp.moveaxis(t, 2, 0) for t in (b_last, C_loc, n_loc, m_loc))
    _, (C_prev, n_prev, m_prev) = lax.scan(step, init, xs)
    C_prev = jnp.moveaxis(C_prev, 0, 2)
    n_prev = jnp.moveaxis(n_prev, 0, 2)
    m_prev = jnp.moveaxis(m_prev, 0, 2)
    lower = jnp.tril(jnp.ones((L, L), dtype=bool))
    D = jnp.where(lower, b[..., :, None] - b[..., None, :] + ig[..., None, :], NEG)
    m_inter = b + m_prev[..., None]
    m_t = jnp.maximum(jnp.max(D, axis=-1), m_inter)
    P = jnp.exp(D - m_t[..., None]) * jnp.einsum('nhctd,nhcsd->nhcts', q, k)
    a = jnp.exp(m_inter - m_t)
    num = jnp.einsum('nhcts,nhcse->nhcte', P, v) + a[..., None] * jnp.einsum('nhctd,nhcde->nhcte', q, C_prev)
    den = jnp.sum(P, axis=-1) + a * jnp.einsum('nhctd,nhcd->nhct', q, n_prev)
    h = num / jnp.maximum(jnp.abs(den), jnp.exp(-m_t))[..., None]
    return h.reshape(N, H, S, dh)


def centred_dwconv(t, w):
    K, C = w.shape
    return lax.conv_general_dilated(t, w[:, None, :], window_strides=(1,), padding=[(K // 2, K // 2)], dimension_numbers=('NWC', 'WIO', 'NWC'), feature_group_count=C)


def mlstm_mixer(qc, kc, vc, oc, gc, conv_w, gate_b, norm_g):
    B, S, _ = qc.shape
    dt = qc.dtype
    qk = jax.nn.silu(centred_dwconv(jnp.concatenate([qc, kc], axis=-1), conv_w))
    q, k = jnp.split(qk, 2, axis=-1)

    def to_bhsd(t):
        return t.reshape(B, S, HEADS_C, HEAD_DIM).transpose(0, 2, 1, 3).astype(jnp.float32)

    q, k, v = to_bhsd(q), to_bhsd(k) * (HEAD_DIM ** -0.5), to_bhsd(vc)
    g = (gc + gate_b).astype(jnp.float32).reshape(B, S, 4, HEADS_C).transpose(2, 0, 3, 1)
    ig_f, fg_f, ig_b, fg_b = g[0], g[1], g[2], g[3]

    def rev(t):
        return jnp.flip(t, axis=2)

    h = mlstm_direction(jnp.concatenate([q, rev(q)], 0), jnp.concatenate([k, rev(k)], 0), jnp.concatenate([v, rev(v)], 0), jnp.concatenate([ig_f, rev(ig_b)], 0), jnp.concatenate([fg_f, rev(fg_b)], 0))
    h = (h[:B] + rev(h[B:])).transpose(0, 2, 1, 3)
    h = h * lax.rsqrt(jnp.mean(h * h, axis=-1, keepdims=True) + EPS) * norm_g.astype(jnp.float32).reshape(HEADS_C, HEAD_DIM)
    return (jax.nn.sigmoid(oc.astype(jnp.float32)) * h.reshape(B, S, HEADS_C * HEAD_DIM)).astype(dt)


def setup_inputs(seed: int = 0) -> dict:
    key = jax.random.key(seed)
    ks = jax.random.split(key, 16)
    f32 = jnp.float32

    def nrm(k, shape, s):
        return jax.random.normal(k, shape, f32) * s

    fb = jnp.linspace(FORGET_BIAS_LO, FORGET_BIAS_HI, HEADS_C, dtype=f32)
    zb = jnp.zeros((HEADS_C,), f32)
    return {
        'x': nrm(ks[0], (BATCH, SEQ, D_MODEL), 1.0),
        'p': nrm(ks[1], (DEPTH, BATCH, SEQ, PLE_DIM), 1.0),
        'rel_bias': nrm(ks[2], (N_BUCKETS, HEADS_A + HEADS_B), 0.2),
        'attn_norm': 1.0 + nrm(ks[3], (DEPTH, D_MODEL), 0.05),
        'w_in': nrm(ks[4], (DEPTH, D_MODEL, D_IN), D_MODEL ** -0.5),
        'qk_conv': nrm(ks[5], (DEPTH, CONV_WIDTH, 2 * HEADS_C * HEAD_DIM), CONV_WIDTH ** -0.5),
        'gate_bias': jnp.concatenate([zb, fb, zb, fb])[None] + nrm(ks[6], (DEPTH, 4 * HEADS_C), 0.1),
        'sink_logits': nrm(ks[7], (DEPTH, HEADS_B), 0.5),
        'mlstm_norm': 1.0 + nrm(ks[8], (DEPTH, HEADS_C * HEAD_DIM), 0.05),
        'w_out': nrm(ks[9], (DEPTH, D_MODEL, D_MODEL), D_MODEL ** -0.5),
        'ffn_norm': 1.0 + nrm(ks[10], (DEPTH, D_MODEL), 0.05),
        'w_up': nrm(ks[11], (DEPTH, D_MODEL, 2 * D_FF), D_MODEL ** -0.5),
        'w_down': nrm(ks[12], (DEPTH, D_FF, D_MODEL), D_FF ** -0.5),
        'ple_proj': nrm(ks[13], (DEPTH, PLE_DIM, D_MODEL), PLE_DIM ** -0.5),
        'ple_gate': nrm(ks[14], (DEPTH, D_MODEL, D_MODEL), D_MODEL ** -0.5),
        'final_norm': 1.0 + nrm(ks[15], (D_MODEL,), 0.05),
    }


def reference(x, p, rel_bias, attn_norm, w_in, qk_conv, gate_bias, sink_logits, mlstm_norm, w_out, ffn_norm, w_up, w_down, ple_proj, ple_gate, final_norm):
    B, S, _ = x.shape
    bias_a = rel_bias[:, :HEADS_A]
    bias_b = rel_bias[:, HEADS_A:]
    for i in range(DEPTH):
        h = rmsnorm(x, attn_norm[i])
        z = h @ w_in[i]
        parts = []
        off = 0
        for sz in COL_SIZES:
            parts.append(z[..., off:off + sz])
            off += sz
        qa, ka, va, qb, kb, vb, qc, kc, vc, oc, gc = parts
        ya = dilated_attention(qa.reshape(B, S, HEADS_A, HEAD_DIM), ka.reshape(B, S, HEADS_A, HEAD_DIM), va.reshape(B, S, HEADS_A, HEAD_DIM), bias_a)
        yb, _ = banded_attention(qb.reshape(B, S, HEADS_B, HEAD_DIM), kb.reshape(B, S, KV_HEADS_B, HEAD_DIM), vb.reshape(B, S, KV_HEADS_B, HEAD_DIM), bias_b, WINDOW_B, WINDOW_B, 1, sink=sink_logits[i])
        yc = mlstm_mixer(qc, kc, vc, oc, gc, qk_conv[i], gate_bias[i], mlstm_norm[i])
        y = jnp.concatenate([ya.reshape(B, S, HEADS_A * HEAD_DIM).astype(x.dtype), yb.reshape(B, S, HEADS_B * HEAD_DIM).astype(x.dtype), yc], axis=-1)
        x = x + y @ w_out[i]
        h = rmsnorm(x, ffn_norm[i])
        gu = h @ w_up[i]
        x = x + (jax.nn.silu(gu[..., :D_FF]) * gu[..., D_FF:]) @ w_down[i]
        x = x + (p[i] @ ple_proj[i]) * jax.nn.sigmoid(x @ ple_gate[i])
    return rmsnorm(x, final_norm)
```

```python
import functools

import jax
import jax.numpy as jnp
from jax import lax
from jax.experimental import pallas as pl
from jax.experimental.pallas import tpu as pltpu

D_MODEL = 1024
HEAD_DIM = 64
HEADS_A = 6
HEADS_B = 4
KV_HEADS_B = 2
HEADS_C = 6
DILATIONS = (1, 4, 16)
HALF_A = 64
HALF_B = 128
CONV_WIDTH = 5
N_BUCKETS = 32
REL_MAX_DIST = 1024
D_FF = 2816
EPS = 1e-6
NEG = -1e30
WA = HEADS_A * HEAD_DIM
WBQ = HEADS_B * HEAD_DIM
WBK = KV_HEADS_B * HEAD_DIM
WC = HEADS_C * HEAD_DIM
D_MAIN = 3 * WA + WBQ + 2 * WBK + 4 * WC

LANES = 128
BF16_ROWS = 16
VMEM_LIMIT = 56 * 1024 * 1024

TM_IN = 512
TM_POST = 512
FF_CHUNK = 256
TQ = 128
TB_A = 2048
TB_B = 1024
LC = 128
HALO = BF16_ROWS

_NT = (((1,), (1,)), ((), ()))
_TN = (((0,), (0,)), ((), ()))
_HI = lax.Precision.HIGHEST


def _rms(x, g):
    return x * lax.rsqrt(jnp.mean(x * x, axis=-1, keepdims=True) + EPS) * g


def _inproj_kernel(x_ref, g_ref, w_ref, wg_ref,
                   a1_ref, a4_ref, a16_ref, qb_ref, kb_ref, vb_ref,
                   qc_ref, kc_ref, vc_ref, oc_ref, gt_ref, z_scr):
    tm = x_ref.shape[0]
    h = _rms(x_ref[...], g_ref[...]).astype(jnp.bfloat16)
    bf = jnp.bfloat16
    za = jnp.dot(h, w_ref[:, 0:3 * WA], preferred_element_type=jnp.float32)
    zr = jnp.dot(h, w_ref[:, 3 * WA:], preferred_element_type=jnp.float32)
    gt = lax.dot_general(wg_ref[...], h, _NT, preferred_element_type=jnp.float32)
    gt_ref[...] = gt.reshape(gt_ref.shape)

    a1_ref[...] = za.astype(bf)
    nl = 3 * WA // LANES
    for j in range(nl):
        z_scr[j] = za[:, j * LANES:(j + 1) * LANES]
    for d, ref in ((4, a4_ref), (16, a16_ref)):
        for c in range(d):
            for j in range(nl):
                ref[c, :, j * LANES:(j + 1) * LANES] = z_scr[j, pl.ds(c, tm // d, stride=d), :].astype(bf)
    o = 0
    qb_ref[...] = zr[:, o:o + WBQ].astype(bf); o += WBQ
    kb_ref[...] = zr[:, o:o + WBK].astype(bf); o += WBK
    vb_ref[...] = zr[:, o:o + WBK].astype(bf); o += WBK
    qc_ref[...] = zr[:, o:o + WC].astype(bf); o += WC
    kc_ref[...] = zr[:, o:o + WC].astype(bf); o += WC
    vc_ref[...] = zr[:, o:o + WC].astype(bf); o += WC
    oc_ref[...] = zr[:, o:o + WC].astype(bf)


def _inproj(x, g, w_main, wg_t):
    B, S, D = x.shape
    tm = TM_IN
    nt = S // tm
    bf = jnp.bfloat16
    f32 = jnp.float32

    def tok(width):
        return pl.BlockSpec((None, tm, width), lambda b, i: (b, i, 0))

    def cls(d):
        return pl.BlockSpec((None, d, tm // d, 3 * WA), lambda b, i: (b, 0, i, 0))

    out_shape = (
        jax.ShapeDtypeStruct((B, S, 3 * WA), bf),
        jax.ShapeDtypeStruct((B, 4, S // 4, 3 * WA), bf),
        jax.ShapeDtypeStruct((B, 16, S // 16, 3 * WA), bf),
        jax.ShapeDtypeStruct((B, S, WBQ), bf),
        jax.ShapeDtypeStruct((B, S, WBK), bf),
        jax.ShapeDtypeStruct((B, S, WBK), bf),
        jax.ShapeDtypeStruct((B, S, WC), bf),
        jax.ShapeDtypeStruct((B, S, WC), bf),
        jax.ShapeDtypeStruct((B, S, WC), bf),
        jax.ShapeDtypeStruct((B, S, WC), bf),
        jax.ShapeDtypeStruct((B, 3, 2, 8, S), f32),
    )
    out_specs = (
        tok(3 * WA), cls(4), cls(16), tok(WBQ), tok(WBK), tok(WBK),
        tok(WC), tok(WC), tok(WC), tok(WC),
        pl.BlockSpec((None, 3, 2, 8, tm), lambda b, i: (b, 0, 0, 0, i)),
    )
    return pl.pallas_call(
        _inproj_kernel,
        grid=(B, nt),
        in_specs=[
            pl.BlockSpec((None, tm, D), lambda b, i: (b, i, 0)),
            pl.BlockSpec((1, D), lambda b, i: (0, 0)),
            pl.BlockSpec((D, D_MAIN), lambda b, i: (0, 0)),
            pl.BlockSpec((48, D), lambda b, i: (0, 0)),
        ],
        out_specs=out_specs,
        out_shape=out_shape,
        scratch_shapes=[pltpu.VMEM((3 * WA // LANES, tm, LANES), f32)],
        compiler_params=pltpu.CompilerParams(
            dimension_semantics=("parallel", "parallel"), vmem_limit_bytes=VMEM_LIMIT),
        name="inproj",
    )(x, g, w_main, wg_t)


def _band_tile(q2, kw, vw, bias0, bias1, lo):
    q2 = q2 * jnp.asarray(HEAD_DIM ** -0.5, q2.dtype)
    vext = jnp.concatenate([vw, jnp.ones_like(vw)], axis=1)
    zero = jnp.zeros_like(q2)
    outs = []
    for hh, bias in ((0, bias0), (1, bias1)):
        qm = jnp.where(lo if hh == 0 else jnp.logical_not(lo), q2, zero)
        s = lax.dot_general(qm, kw, _NT, preferred_element_type=jnp.float32) + bias
        m = jnp.max(s, axis=1, keepdims=True)
        outs.append((m, s))
    return outs, vext


def _attn_a_kernel(q1_ref, q4_ref, q16_ref, k1_ref, v1_ref, k4_ref, v4_ref, k16_ref, v16_ref,
                   bias_ref, o_ref, acc_scr, m_scr, l_scr, *, seq):
    tb = o_ref.shape[0]
    tbi = pl.program_id(2)
    lo = lax.broadcasted_iota(jnp.int32, (TQ, LANES), 1) < HEAD_DIM
    W = TQ + 2 * HALF_A
    groups = ((1, q1_ref, k1_ref, v1_ref), (4, q4_ref, k4_ref, v4_ref), (16, q16_ref, k16_ref, v16_ref))
    for g, (d, q_ref, k_ref, v_ref) in enumerate(groups):
        L = seq // d
        nsub = tb // d
        ntile = nsub // TQ

        def tile(idx, carry, g=g, d=d, q_ref=q_ref, k_ref=k_ref, v_ref=v_ref, L=L, nsub=nsub, ntile=ntile):
            c = idx // ntile
            jj = idx % ntile
            sub0 = tbi * nsub + jj * TQ
            ws = pl.multiple_of(jnp.clip(sub0 - HALF_A, 0, L - W), HALF_A)
            var = (sub0 - ws) // HALF_A
            q2 = q_ref[c, pl.ds(pl.multiple_of(jj * TQ, TQ), TQ), :]
            kw = k_ref[c, pl.ds(ws, W), :]
            vw = v_ref[c, pl.ds(ws, W), :]
            (hm0, hm1), vext = _band_tile(q2, kw, vw, bias_ref[g, var, 0], bias_ref[g, var, 1], lo)
            r0 = jnp.dot(jnp.exp(hm0[1] - hm0[0]).astype(jnp.bfloat16), vext, preferred_element_type=jnp.float32)
            r1 = jnp.dot(jnp.exp(hm1[1] - hm1[0]).astype(jnp.bfloat16), vext, preferred_element_type=jnp.float32)
            rows = pl.ds(jj * (TQ * d) + c, TQ, stride=d) if d > 1 else pl.ds(pl.multiple_of(jj * TQ, TQ), TQ)
            acc_scr[g, rows, :] = jnp.where(lo, r0[:, :LANES], r1[:, :LANES])
            l_scr[g, rows, :] = jnp.where(lo, r0[:, LANES:], r1[:, LANES:])
            m_scr[g, rows, :] = jnp.where(lo, hm0[0], hm1[0])
            return carry

        lax.fori_loop(0, d * ntile, tile, 0)

    rc = 256

    def combine(i, carry):
        rows = pl.ds(pl.multiple_of(i * rc, rc), rc)
        m0, m1, m2 = m_scr[0, rows, :], m_scr[1, rows, :], m_scr[2, rows, :]
        mm = jnp.maximum(jnp.maximum(m0, m1), m2)
        e0, e1, e2 = jnp.exp(m0 - mm), jnp.exp(m1 - mm), jnp.exp(m2 - mm)
        num = e0 * acc_scr[0, rows, :] + e1 * acc_scr[1, rows, :] + e2 * acc_scr[2, rows, :]
        den = e0 * l_scr[0, rows, :] + e1 * l_scr[1, rows, :] + e2 * l_scr[2, rows, :]
        o_ref[rows, :] = (num / den).astype(o_ref.dtype)
        return carry

    lax.fori_loop(0, tb // rc, combine, 0)


def _attn_a(a1, a4, a16, bias):
    B, _, S, _ = a1.shape
    tb = TB_A
    npair = HEADS_A // 2

    def qspec(d):
        return pl.BlockSpec((None, d, tb // d, LANES), lambda b, p, t: (b, 0, t, p))

    def kvspec(d, off):
        return pl.BlockSpec((None, d, S // d, LANES), lambda b, p, t, off=off: (b, 0, 0, off + p))

    in_specs = [qspec(1), qspec(4), qspec(16)]
    args = [a1, a4, a16]
    for d, arr in ((1, a1), (4, a4), (16, a16)):
        in_specs += [kvspec(d, npair), kvspec(d, 2 * npair)]
        args += [arr, arr]
    in_specs.append(pl.BlockSpec((3, 3, 2, TQ, TQ + 2 * HALF_A), lambda b, p, t: (0, 0, p, 0, 0)))
    args.append(bias)
    return pl.pallas_call(
        functools.partial(_attn_a_kernel, seq=S),
        grid=(B, npair, S // tb),
        in_specs=in_specs,
        out_specs=pl.BlockSpec((None, tb, LANES), lambda b, p, t: (b, t, p)),
        out_shape=jax.ShapeDtypeStruct((B, S, WA), jnp.bfloat16),
        scratch_shapes=[pltpu.VMEM((3, tb, LANES), jnp.float32)] * 3,
        compiler_params=pltpu.CompilerParams(
            dimension_semantics=("parallel", "parallel", "arbitrary"), vmem_limit_bytes=VMEM_LIMIT),
        name="attn_a",
    )(*args)


def _attn_b_kernel(q_ref, k_ref, v_ref, bias_ref, sink_ref, o_ref, *, seq):
    tb = o_ref.shape[0]
    tbi = pl.program_id(2)
    lo = lax.broadcasted_iota(jnp.int32, (TQ, LANES), 1) < HEAD_DIM
    W = TQ + 2 * HALF_B
    s0 = sink_ref[0, 0:1, 0:1]
    s1 = sink_ref[1, 0:1, 0:1]

    def tile(jj, carry):
        sub0 = tbi * tb + jj * TQ
        ws = pl.multiple_of(jnp.clip(sub0 - HALF_B, 0, seq - W), HALF_B)
        var = (sub0 - ws) // HALF_B
        rows = pl.ds(pl.multiple_of(jj * TQ, TQ), TQ)
        q2 = q_ref[rows, :]
        kw = k_ref[pl.ds(ws, W), :]
        vw = v_ref[pl.ds(ws, W), :]
        (hm0, hm1), vext = _band_tile(q2, kw, vw, bias_ref[var, 0], bias_ref[var, 1], lo)
        m0 = jnp.maximum(hm0[0], s0)
        m1 = jnp.maximum(hm1[0], s1)
        r0 = jnp.dot(jnp.exp(hm0[1] - m0).astype(jnp.bfloat16), vext, preferred_element_type=jnp.float32)
        r1 = jnp.dot(jnp.exp(hm1[1] - m1).astype(jnp.bfloat16), vext, preferred_element_type=jnp.float32)
        num = jnp.where(lo, r0[:, :LANES], r1[:, :LANES])
        den = jnp.where(lo, r0[:, LANES:] + jnp.exp(s0 - m0), r1[:, LANES:] + jnp.exp(s1 - m1))
        o_ref[rows, :] = (num / den).astype(o_ref.dtype)
        return carry

    lax.fori_loop(0, tb // TQ, tile, 0)


def _attn_b(qb, kb, vb, bias, sink):
    B, S, _ = qb.shape
    tb = TB_B
    npair = HEADS_B // 2
    return pl.pallas_call(
        functools.partial(_attn_b_kernel, seq=S),
        grid=(B, npair, S // tb),
        in_specs=[
            pl.BlockSpec((None, tb, LANES), lambda b, p, t: (b, t, p)),
            pl.BlockSpec((None, S, LANES), lambda b, p, t: (b, 0, 0)),
            pl.BlockSpec((None, S, LANES), lambda b, p, t: (b, 0, 0)),
            pl.BlockSpec((3, None, 2, TQ, TQ + 2 * HALF_B), lambda b, p, t: (0, p, 0, 0, 0)),
            pl.BlockSpec((None, 2, 8, LANES), lambda b, p, t: (p, 0, 0, 0)),
        ],
        out_specs=pl.BlockSpec((None, tb, LANES), lambda b, p, t: (b, t, p)),
        out_shape=jax.ShapeDtypeStruct((B, S, WBQ), jnp.bfloat16),
        compiler_params=pltpu.CompilerParams(
            dimension_semantics=("parallel", "parallel", "arbitrary"), vmem_limit_bytes=VMEM_LIMIT),
        name="attn_b",
    )(qb, kb, vb, bias, sink)


def _mlstm_kernel(qp_ref, qm_ref, qn_ref, kp_ref, km_ref, kn_ref, v_ref, o_ref, gt_ref, gb_ref,
                  cq_ref, ck_ref, ng_ref, tri_ref, y_ref,
                  xq_scr, xk_scr, st_scr, m_scr, hf_scr, *, nc):
    drn = pl.program_id(2)
    c = pl.program_id(3)
    ce = jnp.where(drn == 0, c, nc - 1 - c)
    f32 = jnp.float32
    bf = jnp.bfloat16
    lane = lax.broadcasted_iota(jnp.int32, (LC, LANES), 1)
    lo = lane < HEAD_DIM

    @pl.when(c == 0)
    def _():
        st_scr[...] = jnp.zeros_like(st_scr)
        m_scr[...] = jnp.zeros_like(m_scr)

    keep_prev = (ce > 0).astype(f32)
    keep_next = (ce < nc - 1).astype(f32)

    def conv_silu(p_ref, m_ref, n_ref, x_scr, w_ref):
        x_scr[0:HALO, :] = p_ref[...].astype(f32) * keep_prev
        x_scr[HALO:HALO + LC, :] = m_ref[...].astype(f32)
        x_scr[HALO + LC:, :] = n_ref[...].astype(f32) * keep_next
        acc = jnp.zeros((LC, LANES), f32)
        for j in range(CONV_WIDTH):
            acc = acc + x_scr[pl.ds(HALO - CONV_WIDTH // 2 + j, LC), :] * w_ref[j:j + 1, :]
        return acc * jax.nn.sigmoid(acc)

    q2 = conv_silu(qp_ref, qm_ref, qn_ref, xq_scr, cq_ref).astype(bf)
    k2 = (conv_silu(kp_ref, km_ref, kn_ref, xk_scr, ck_ref) * (HEAD_DIM ** -0.5)).astype(bf)
    v2 = v_ref[...]
    vext = jnp.concatenate([v2, jnp.ones_like(v2)], axis=1)

    gt = gt_ref[...] + gb_ref[...]
    lf = jnp.minimum(gt, 0.0) - jnp.log(1.0 + jnp.exp(-jnp.abs(gt)))
    tri = tri_ref[...]
    eye = (lax.broadcasted_iota(jnp.int32, (LC, LC), 0) == lax.broadcasted_iota(jnp.int32, (LC, LC), 1)).astype(f32)
    b_row = lax.dot_general(lf, tri, _NT, precision=_HI, preferred_element_type=f32)
    b_col = lax.dot_general(tri, lf, _NT, precision=_HI, preferred_element_type=f32)
    g_col = lax.dot_general(eye, gt, _NT, precision=_HI, preferred_element_type=f32)
    b_tot = jnp.sum(lf, axis=1, keepdims=True)
    visible = tri > 0.0

    q2z = jnp.zeros_like(q2)
    res = []
    for hh in (0, 1):
        bc = b_col[:, 2 + hh:3 + hh]
        br = b_row[2 + hh:3 + hh, :]
        ir = gt[hh:hh + 1, :]
        ic = g_col[:, hh:hh + 1]
        bt = b_tot[2 + hh:3 + hh, :]
        mp = m_scr[hh:hh + 1, 0:1]
        dmat = jnp.where(visible, bc - br + ir, NEG)
        m_in = bc + mp
        m_t = jnp.maximum(jnp.max(dmat, axis=1, keepdims=True), m_in)
        qm = jnp.where(lo if hh == 0 else jnp.logical_not(lo), q2, q2z)
        qk = lax.dot_general(qm, k2, _NT, preferred_element_type=f32)
        p = (jnp.exp(dmat - m_t) * qk).astype(bf)
        r = jnp.dot(p, vext, preferred_element_type=f32)
        a_t = jnp.exp(m_in - m_t)
        m_loc = jnp.max(bt - br + ir, axis=1, keepdims=True)
        m_new = jnp.maximum(bt + mp, m_loc)
        e_col = jnp.exp(bt - bc + ic - m_new)
        decay = jnp.exp(bt + mp - m_new)
        res.append((r, a_t, m_t, e_col, decay, m_new))

    (r0, a0, mt0, e0, d0, mn0), (r1, a1, mt1, e1, d1, mn1) = res
    state = st_scr[...]
    inter = jnp.dot(q2, state.astype(bf), preferred_element_type=f32)
    a2 = jnp.where(lo, a0, a1)
    mt2 = jnp.where(lo, mt0, mt1)
    num = jnp.where(lo, r0[:, :LANES], r1[:, :LANES]) + a2 * inter[:, :LANES]
    den = jnp.where(lo, r0[:, LANES:], r1[:, LANES:]) + a2 * inter[:, LANES:]
    h = num / jnp.maximum(jnp.abs(den), jnp.exp(-mt2))

    ke = (k2.astype(f32) * jnp.where(lo, e0, e1)).astype(bf)
    s_loc = lax.dot_general(ke, vext, _TN, preferred_element_type=f32)
    row = lax.broadcasted_iota(jnp.int32, (LANES, 2 * LANES), 0)
    col = lax.broadcasted_iota(jnp.int32, (LANES, 2 * LANES), 1)
    same_head = (row < HEAD_DIM) == ((col % LANES) < HEAD_DIM)
    dec_rows = jnp.where(row < HEAD_DIM, d0, d1)
    st_scr[...] = dec_rows * state + jnp.where(same_head, s_loc, 0.0)
    m_scr[0:1, :] = jnp.broadcast_to(mn0, (1, LANES))
    m_scr[1:2, :] = jnp.broadcast_to(mn1, (1, LANES))

    @pl.when(drn == 0)
    def _():
        hf_scr[ce] = h

    @pl.when(drn == 1)
    def _():
        hs = h + hf_scr[ce]
        hd = (lax.broadcasted_iota(jnp.int32, (LANES, LANES), 0) < HEAD_DIM) == \
             (lax.broadcasted_iota(jnp.int32, (LANES, LANES), 1) < HEAD_DIM)
        ms = jnp.dot(hs * hs, hd.astype(f32) * (1.0 / HEAD_DIM), precision=_HI, preferred_element_type=f32)
        hn = hs * lax.rsqrt(ms + EPS) * ng_ref[0:1, :]
        y_ref[...] = (jax.nn.sigmoid(o_ref[...].astype(f32)) * hn).astype(y_ref.dtype)


def _mlstm(qc, kc, vc, oc, gt, gate_b, conv_q, conv_k, norm_g, tri):
    B, S, _ = qc.shape
    nc = S // LC
    npair = HEADS_C // 2
    hpc = LC // HALO
    nh = S // HALO

    def ce(d, c):
        return jnp.where(d == 0, c, nc - 1 - c)

    def cur():
        return pl.BlockSpec((None, LC, LANES), lambda b, p, d, c: (b, ce(d, c), p))

    def prev():
        return pl.BlockSpec((None, HALO, LANES), lambda b, p, d, c: (b, jnp.maximum(ce(d, c) * hpc - 1, 0), p))

    def nxt():
        return pl.BlockSpec((None, HALO, LANES), lambda b, p, d, c: (b, jnp.minimum((ce(d, c) + 1) * hpc, nh - 1), p))

    def per_pair(rows):
        return pl.BlockSpec((rows, LANES), lambda b, p, d, c: (0, p))

    in_specs = [
        prev(), cur(), nxt(), prev(), cur(), nxt(), cur(), cur(),
        pl.BlockSpec((None, None, None, 8, LC), lambda b, p, d, c: (b, p, d, 0, ce(d, c))),
        pl.BlockSpec((None, None, 8, LC), lambda b, p, d, c: (p, d, 0, 0)),
        per_pair(8), per_pair(8), per_pair(8),
        pl.BlockSpec((None, LC, LC), lambda b, p, d, c: (d, 0, 0)),
    ]
    return pl.pallas_call(
        functools.partial(_mlstm_kernel, nc=nc),
        grid=(B, npair, 2, nc),
        in_specs=in_specs,
        out_specs=pl.BlockSpec((None, LC, LANES), lambda b, p, d, c: (b, jnp.where(d == 0, nc - 1, nc - 1 - c), p)),
        out_shape=jax.ShapeDtypeStruct((B, S, WC), jnp.bfloat16),
        scratch_shapes=[
            pltpu.VMEM((LC + 2 * HALO, LANES), jnp.float32),
            pltpu.VMEM((LC + 2 * HALO, LANES), jnp.float32),
            pltpu.VMEM((LANES, 2 * LANES), jnp.float32),
            pltpu.VMEM((8, LANES), jnp.float32),
            pltpu.VMEM((nc, LC, LANES), jnp.float32),
        ],
        compiler_params=pltpu.CompilerParams(
            dimension_semantics=("parallel", "parallel", "arbitrary", "arbitrary"), vmem_limit_bytes=VMEM_LIMIT),
        name="mlstm",
    )(qc, qc, qc, kc, kc, kc, vc, oc, gt, gate_b, conv_q, conv_k, norm_g, tri)


def _post_kernel(x_ref, ya_ref, yb_ref, yc_ref, p_ref, wo_ref, fg_ref,
                 wup_ref, wdn_ref, pp_ref, pg_ref, fin_ref, out_ref, *, final):
    f32 = jnp.float32
    bf = jnp.bfloat16
    y = jnp.concatenate([ya_ref[...], yb_ref[...], yc_ref[...]], axis=1)
    x1 = x_ref[...] + jnp.dot(y, wo_ref[...], preferred_element_type=f32)
    h = _rms(x1, fg_ref[...]).astype(bf)
    ffn = None
    for j in range(D_FF // FF_CHUNK):
        gate = jnp.dot(h, wup_ref[:, j * FF_CHUNK:(j + 1) * FF_CHUNK], preferred_element_type=f32)
        up = jnp.dot(h, wup_ref[:, D_FF + j * FF_CHUNK:D_FF + (j + 1) * FF_CHUNK], preferred_element_type=f32)
        act = (gate * jax.nn.sigmoid(gate) * up).astype(bf)
        part = jnp.dot(act, wdn_ref[j * FF_CHUNK:(j + 1) * FF_CHUNK, :], preferred_element_type=f32)
        ffn = part if ffn is None else ffn + part
    x2 = x1 + ffn
    emb = jnp.dot(p_ref[...].astype(bf), pp_ref[...], preferred_element_type=f32)
    gate = jax.nn.sigmoid(jnp.dot(x2.astype(bf), pg_ref[...], preferred_element_type=f32))
    x3 = x2 + emb * gate
    if final:
        x3 = _rms(x3, fin_ref[...])
    out_ref[...] = x3


def _post(x, ya, yb, yc, p, wo, fg, wup, wdn, pp, pg, fin, final):
    B, S, D = x.shape
    tm = TM_POST

    def tok(width):
        return pl.BlockSpec((None, tm, width), lambda b, i: (b, i, 0))

    def const(shape):
        return pl.BlockSpec(shape, lambda b, i: (0,) * len(shape), pipeline_mode=pl.Buffered(1))

    return pl.pallas_call(
        functools.partial(_post_kernel, final=final),
        grid=(B, S // tm),
        in_specs=[
            tok(D), tok(WA), tok(WBQ), tok(WC), tok(p.shape[-1]),
            const(wo.shape), const((1, D)),
            const(wup.shape), const(wdn.shape), const(pp.shape), const(pg.shape), const((1, D)),
        ],
        out_specs=tok(D),
        out_shape=jax.ShapeDtypeStruct((B, S, D), jnp.float32),
        compiler_params=pltpu.CompilerParams(
            dimension_semantics=("parallel", "parallel"), vmem_limit_bytes=VMEM_LIMIT),
        name="post",
    )(x, ya, yb, yc, p, wo, fg, wup, wdn, pp, pg, fin)


def _t5_bucket(rel):
    half_b = N_BUCKETS // 2
    max_exact = half_b // 2
    n = jnp.abs(rel)
    nf = jnp.maximum(n, 1).astype(jnp.float32)
    large = max_exact + (jnp.log(nf / max_exact) / jnp.log(jnp.float32(REL_MAX_DIST / max_exact))
                         * (half_b - max_exact)).astype(jnp.int32)
    large = jnp.minimum(large, half_b - 1)
    return jnp.where(rel > 0, half_b, 0) + jnp.where(n < max_exact, n, large)


def _band_bias(table, half, dil):
    W = TQ + 2 * half
    iq = jnp.arange(TQ, dtype=jnp.int32)[:, None]
    ik = jnp.arange(W, dtype=jnp.int32)[None, :]
    tiles = []
    for shift in (0, half, 2 * half):
        rel = ik - iq - shift
        vals = jnp.transpose(table[_t5_bucket(rel * dil)], (2, 0, 1)).astype(jnp.float32)
        tiles.append(jnp.where((jnp.abs(rel) <= half)[None], vals, NEG))
    return jnp.stack(tiles, 0)


_B_ORDER = (0, 2, 1, 3)


def _perm_heads(w, order, axis):
    blocks = [lax.slice_in_dim(w, h * HEAD_DIM, (h + 1) * HEAD_DIM, axis=axis) for h in order]
    return jnp.concatenate(blocks, axis=axis)


def kernel(x, p, rel_bias, attn_norm, w_in, qk_conv, gate_bias, sink_logits, mlstm_norm, w_out, ffn_norm,
           w_up, w_down, ple_proj, ple_gate, final_norm):
    B, S, D = x.shape
    depth = w_in.shape[0]
    assert D == D_MODEL and S % TB_A == 0 and S // 16 >= TQ + 2 * HALF_A
    f32 = jnp.float32
    bf = jnp.bfloat16

    bias_a = jnp.stack([_band_bias(rel_bias[:, :HEADS_A], HALF_A, d) for d in DILATIONS], 0)
    bias_b = _band_bias(rel_bias[:, HEADS_A:], HALF_B, 1)
    bias_b = jnp.stack([bias_b[:, h] for h in _B_ORDER], 1).reshape(3, 2, 2, TQ, TQ + 2 * HALF_B)
    tri_lo = (jnp.arange(LC)[:, None] >= jnp.arange(LC)[None, :]).astype(f32)
    tri = jnp.stack([tri_lo, tri_lo.T], 0)

    gate_off = D_MAIN
    for i in range(depth):
        w = w_in[i]
        qb_off = 3 * WA
        w_main = jnp.concatenate([
            w[:, :qb_off], _perm_heads(w[:, qb_off:qb_off + WBQ], _B_ORDER, 1), w[:, qb_off + WBQ:gate_off]],
            axis=1).astype(bf)
        wg = w[:, gate_off:]
        gb = gate_bias[i]
        rows, brow = [], []
        zrow = jnp.zeros((4, D), f32)
        for pr in range(HEADS_C // 2):
            for drn in range(2):
                idx = [(2 * drn) * HEADS_C + 2 * pr, (2 * drn) * HEADS_C + 2 * pr + 1,
                       (2 * drn + 1) * HEADS_C + 2 * pr, (2 * drn + 1) * HEADS_C + 2 * pr + 1]
                rows += [wg[:, idx].T, zrow]
                brow += [gb[jnp.array(idx)], jnp.zeros((4,), f32)]
        wg_t = jnp.concatenate(rows, 0).astype(bf)
        gate_b = jnp.broadcast_to(jnp.concatenate(brow).reshape(3, 2, 8, 1), (3, 2, 8, LC))
        conv = jnp.concatenate([qk_conv[i], jnp.zeros((8 - CONV_WIDTH, 2 * WC), f32)], 0)
        conv_q, conv_k = conv[:, :WC], conv[:, WC:]
        norm_g = jnp.broadcast_to(mlstm_norm[i][None], (8, WC))
        sink = jnp.broadcast_to(sink_logits[i][jnp.array(_B_ORDER)].reshape(2, 2, 1, 1), (2, 2, 8, LANES))
        wo = w_out[i]
        wo = jnp.concatenate([wo[:WA], _perm_heads(wo[WA:WA + WBQ], _B_ORDER, 0), wo[WA + WBQ:]], axis=0).astype(bf)

        a1, a4, a16, qb, kb, vb, qc, kc, vc, oc, gt = _inproj(x, attn_norm[i][None], w_main, wg_t)
        ya = _attn_a(a1.reshape(B, 1, S, 3 * WA), a4, a16, bias_a)
        yb = _attn_b(qb, kb, vb, bias_b, sink)
        yc = _mlstm(qc, kc, vc, oc, gt, gate_b, conv_q, conv_k, norm_g, tri)
        x = _post(x, ya, yb, yc, p[i], wo, ffn_norm[i][None], w_up[i].astype(bf), w_down[i].astype(bf),
                  ple_proj[i].astype(bf), ple_gate[i].astype(bf), final_norm[None], final=(i == depth - 1))
    return x
```

```python
import functools

import jax
import jax.numpy as jnp
from jax import lax
from jax.experimental import pallas as pl
from jax.experimental.pallas import tpu as pltpu

D_MODEL = 1024
HEAD_DIM = 64
HEADS_A = 6
HEADS_B = 4
KV_HEADS_B = 2
HEADS_C = 6
DILATIONS = (1, 4, 16)
HALF_A = 64
HALF_B = 128
CONV_WIDTH = 5
N_BUCKETS = 32
REL_MAX_DIST = 1024
D_FF = 2816
EPS = 1e-6
NEG = -1e30
WA = HEADS_A * HEAD_DIM
WBQ = HEADS_B * HEAD_DIM
WBK = KV_HEADS_B * HEAD_DIM
WC = HEADS_C * HEAD_DIM
D_MAIN = 3 * WA + WBQ + 2 * WBK + 4 * WC

LANES = 128
BF16_ROWS = 16
VMEM_LIMIT = 56 * 1024 * 1024

TM_IN = 512
TM_POST = 512
FF_CHUNK = 256
TQ = 128
TB_A = 2048
TB_B = 1024
LC = 128
HALO = BF16_ROWS

_NT = (((1,), (1,)), ((), ()))
_TN = (((0,), (0,)), ((), ()))
_HI = lax.Precision.HIGHEST


def _rms(x, g):
    return x * lax.rsqrt(jnp.mean(x * x, axis=-1, keepdims=True) + EPS) * g


def _inproj_kernel(x_ref, g_ref, w_ref, wg_ref,
                   a1_ref, a4_ref, a16_ref, qb_ref, kb_ref, vb_ref,
                   qc_ref, kc_ref, vc_ref, oc_ref, gt_ref, z_scr):
    tm = x_ref.shape[0]
    h = _rms(x_ref[...], g_ref[...]).astype(jnp.bfloat16)
    bf = jnp.bfloat16
    za = jnp.dot(h, w_ref[:, 0:3 * WA], preferred_element_type=jnp.float32)
    zr = jnp.dot(h, w_ref[:, 3 * WA:], preferred_element_type=jnp.float32)
    gt = lax.dot_general(wg_ref[...], h, _NT, preferred_element_type=jnp.float32)
    gt_ref[...] = gt.reshape(gt_ref.shape)

    a1_ref[...] = za.astype(bf)
    nl = 3 * WA // LANES
    for j in range(nl):
        z_scr[j] = za[:, j * LANES:(j + 1) * LANES]
    for d, ref in ((4, a4_ref), (16, a16_ref)):
        for c in range(d):
            for j in range(nl):
                ref[c, :, j * LANES:(j + 1) * LANES] = z_scr[j, pl.ds(c, tm // d, stride=d), :].astype(bf)
    o = 0
    qb_ref[...] = zr[:, o:o + WBQ].astype(bf); o += WBQ
    kb_ref[...] = zr[:, o:o + WBK].astype(bf); o += WBK
    vb_ref[...] = zr[:, o:o + WBK].astype(bf); o += WBK
    qc_ref[...] = zr[:, o:o + WC].astype(bf); o += WC
    kc_ref[...] = zr[:, o:o + WC].astype(bf); o += WC
    vc_ref[...] = zr[:, o:o + WC].astype(bf); o += WC
    oc_ref[...] = zr[:, o:o + WC].astype(bf)


def _inproj(x, g, w_main, wg_t):
    B, S, D = x.shape
    tm = TM_IN
    nt = S // tm
    bf = jnp.bfloat16
    f32 = jnp.float32

    def tok(width):
        return pl.BlockSpec((None, tm, width), lambda b, i: (b, i, 0))

    def cls(d):
        return pl.BlockSpec((None, d, tm // d, 3 * WA), lambda b, i: (b, 0, i, 0))

    out_shape = (
        jax.ShapeDtypeStruct((B, S, 3 * WA), bf),
        jax.ShapeDtypeStruct((B, 4, S // 4, 3 * WA), bf),
        jax.ShapeDtypeStruct((B, 16, S // 16, 3 * WA), bf),
        jax.ShapeDtypeStruct((B, S, WBQ), bf),
        jax.ShapeDtypeStruct((B, S, WBK), bf),
        jax.ShapeDtypeStruct((B, S, WBK), bf),
        jax.ShapeDtypeStruct((B, S, WC), bf),
        jax.ShapeDtypeStruct((B, S, WC), bf),
        jax.ShapeDtypeStruct((B, S, WC), bf),
        jax.ShapeDtypeStruct((B, S, WC), bf),
        jax.ShapeDtypeStruct((B, 3, 2, 8, S), f32),
    )
    out_specs = (
        tok(3 * WA), cls(4), cls(16), tok(WBQ), tok(WBK), tok(WBK),
        tok(WC), tok(WC), tok(WC), tok(WC),
        pl.BlockSpec((None, 3, 2, 8, tm), lambda b, i: (b, 0, 0, 0, i)),
    )
    return pl.pallas_call(
        _inproj_kernel,
        grid=(B, nt),
        in_specs=[
            pl.BlockSpec((None, tm, D), lambda b, i: (b, i, 0)),
            pl.BlockSpec((1, D), lambda b, i: (0, 0)),
            pl.BlockSpec((D, D_MAIN), lambda b, i: (0, 0)),
            pl.BlockSpec((48, D), lambda b, i: (0, 0)),
        ],
        out_specs=out_specs,
        out_shape=out_shape,
        scratch_shapes=[pltpu.VMEM((3 * WA // LANES, tm, LANES), f32)],
        compiler_params=pltpu.CompilerParams(
            dimension_semantics=("parallel", "parallel"), vmem_limit_bytes=VMEM_LIMIT),
        name="inproj",
    )(x, g, w_main, wg_t)


def _band_tile(q2, kw, vw, bias0, bias1, lo):
    q2 = q2 * jnp.asarray(HEAD_DIM ** -0.5, q2.dtype)
    vext = jnp.concatenate([vw, jnp.ones_like(vw)], axis=1)
    zero = jnp.zeros_like(q2)
    outs = []
    for hh, bias in ((0, bias0), (1, bias1)):
        qm = jnp.where(lo if hh == 0 else jnp.logical_not(lo), q2, zero)
        s = lax.dot_general(qm, kw, _NT, preferred_element_type=jnp.float32) + bias
        m = jnp.max(s, axis=1, keepdims=True)
        outs.append((m, s))
    return outs, vext


def _attn_a_kernel(q1_ref, q4_ref, q16_ref, k1_ref, v1_ref, k4_ref, v4_ref, k16_ref, v16_ref,
                   bias_ref, o_ref, acc_scr, m_scr, l_scr, *, seq):
    tb = o_ref.shape[0]
    tbi = pl.program_id(2)
    lo = lax.broadcasted_iota(jnp.int32, (TQ, LANES), 1) < HEAD_DIM
    W = TQ + 2 * HALF_A
    groups = ((1, q1_ref, k1_ref, v1_ref), (4, q4_ref, k4_ref, v4_ref), (16, q16_ref, k16_ref, v16_ref))
    for g, (d, q_ref, k_ref, v_ref) in enumerate(groups):
        L = seq // d
        nsub = tb // d
        ntile = nsub // TQ

        def tile(idx, carry, g=g, d=d, q_ref=q_ref, k_ref=k_ref, v_ref=v_ref, L=L, nsub=nsub, ntile=ntile):
            c = idx // ntile
            jj = idx % ntile
            sub0 = tbi * nsub + jj * TQ
            ws = pl.multiple_of(jnp.clip(sub0 - HALF_A, 0, L - W), HALF_A)
            var = (sub0 - ws) // HALF_A
            q2 = q_ref[c, pl.ds(pl.multiple_of(jj * TQ, TQ), TQ), :]
            kw = k_ref[c, pl.ds(ws, W), :]
            vw = v_ref[c, pl.ds(ws, W), :]
            (hm0, hm1), vext = _band_tile(q2, kw, vw, bias_ref[g, var, 0], bias_ref[g, var, 1], lo)
            r0 = jnp.dot(jnp.exp(hm0[1] - hm0[0]).astype(jnp.bfloat16), vext, preferred_element_type=jnp.float32)
            r1 = jnp.dot(jnp.exp(hm1[1] - hm1[0]).astype(jnp.bfloat16), vext, preferred_element_type=jnp.float32)
            rows = pl.ds(jj * (TQ * d) + c, TQ, stride=d) if d > 1 else pl.ds(pl.multiple_of(jj * TQ, TQ), TQ)
            acc_scr[g, rows, :] = jnp.where(lo, r0[:, :LANES], r1[:, :LANES])
            l_scr[g, rows, :] = jnp.where(lo, r0[:, LANES:], r1[:, LANES:])
            m_scr[g, rows, :] = jnp.where(lo, hm0[0], hm1[0])
            return carry

        lax.fori_loop(0, d * ntile, tile, 0)

    rc = 256

    def combine(i, carry):
        rows = pl.ds(pl.multiple_of(i * rc, rc), rc)
        m0, m1, m2 = m_scr[0, rows, :], m_scr[1, rows, :], m_scr[2, rows, :]
        mm = jnp.maximum(jnp.maximum(m0, m1), m2)
        e0, e1, e2 = jnp.exp(m0 - mm), jnp.exp(m1 - mm), jnp.exp(m2 - mm)
        num = e0 * acc_scr[0, rows, :] + e1 * acc_scr[1, rows, :] + e2 * acc_scr[2, rows, :]
        den = e0 * l_scr[0, rows, :] + e1 * l_scr[1, rows, :] + e2 * l_scr[2, rows, :]
        o_ref[rows, :] = (num / den).astype(o_ref.dtype)
        return carry

    lax.fori_loop(0, tb // rc, combine, 0)


def _attn_a(a1, a4, a16, bias):
    B, _, S, _ = a1.shape
    tb = TB_A
    npair = HEADS_A // 2

    def qspec(d):
        return pl.BlockSpec((None, d, tb // d, LANES), lambda b, p, t: (b, 0, t, p))

    def kvspec(d, off):
        return pl.BlockSpec((None, d, S // d, LANES), lambda b, p, t, off=off: (b, 0, 0, off + p))

    in_specs = [qspec(1), qspec(4), qspec(16)]
    args = [a1, a4, a16]
    for d, arr in ((1, a1), (4, a4), (16, a16)):
        in_specs += [kvspec(d, npair), kvspec(d, 2 * npair)]
        args += [arr, arr]
    in_specs.append(pl.BlockSpec((3, 3, 2, TQ, TQ + 2 * HALF_A), lambda b, p, t: (0, 0, p, 0, 0)))
    args.append(bias)
    return pl.pallas_call(
        functools.partial(_attn_a_kernel, seq=S),
        grid=(B, npair, S // tb),
        in_specs=in_specs,
        out_specs=pl.BlockSpec((None, tb, LANES), lambda b, p, t: (b, t, p)),
        out_shape=jax.ShapeDtypeStruct((B, S, WA), jnp.bfloat16),
        scratch_shapes=[pltpu.VMEM((3, tb, LANES), jnp.float32)] * 3,
        compiler_params=pltpu.CompilerParams(
            dimension_semantics=("parallel", "parallel", "arbitrary"), vmem_limit_bytes=VMEM_LIMIT),
        name="attn_a",
    )(*args)


def _attn_b_kernel(q_ref, k_ref, v_ref, bias_ref, sink_ref, o_ref, *, seq):
    tb = o_ref.shape[0]
    tbi = pl.program_id(2)
    lo = lax.broadcasted_iota(jnp.int32, (TQ, LANES), 1) < HEAD_DIM
    W = TQ + 2 * HALF_B
    s0 = sink_ref[0, 0:1, 0:1]
    s1 = sink_ref[1, 0:1, 0:1]

    def tile(jj, carry):
        sub0 = tbi * tb + jj * TQ
        ws = pl.multiple_of(jnp.clip(sub0 - HALF_B, 0, seq - W), HALF_B)
        var = (sub0 - ws) // HALF_B
        rows = pl.ds(pl.multiple_of(jj * TQ, TQ), TQ)
        q2 = q_ref[rows, :]
        kw = k_ref[pl.ds(ws, W), :]
        vw = v_ref[pl.ds(ws, W), :]
        (hm0, hm1), vext = _band_tile(q2, kw, vw, bias_ref[var, 0], bias_ref[var, 1], lo)
        m0 = jnp.maximum(hm0[0], s0)
        m1 = jnp.maximum(hm1[0], s1)
        r0 = jnp.dot(jnp.exp(hm0[1] - m0).astype(jnp.bfloat16), vext, preferred_element_type=jnp.float32)
        r1 = jnp.dot(jnp.exp(hm1[1] - m1).astype(jnp.bfloat16), vext, preferred_element_type=jnp.float32)
        num = jnp.where(lo, r0[:, :LANES], r1[:, :LANES])
        den = jnp.where(lo, r0[:, LANES:] + jnp.exp(s0 - m0), r1[:, LANES:] + jnp.exp(s1 - m1))
        o_ref[rows, :] = (num / den).astype(o_ref.dtype)
        return carry

    lax.fori_loop(0, tb // TQ, tile, 0)


def _attn_b(qb, kb, vb, bias, sink):
    B, S, _ = qb.shape
    tb = TB_B
    npair = HEADS_B // 2
    return pl.pallas_call(
        functools.partial(_attn_b_kernel, seq=S),
        grid=(B, npair, S // tb),
        in_specs=[
            pl.BlockSpec((None, tb, LANES), lambda b, p, t: (b, t, p)),
            pl.BlockSpec((None, S, LANES), lambda b, p, t: (b, 0, 0)),
            pl.BlockSpec((None, S, LANES), lambda b, p, t: (b, 0, 0)),
            pl.BlockSpec((3, None, 2, TQ, TQ + 2 * HALF_B), lambda b, p, t: (0, p, 0, 0, 0)),
            pl.BlockSpec((None, 2, 8, LANES), lambda b, p, t: (p, 0, 0, 0)),
        ],
        out_specs=pl.BlockSpec((None, tb, LANES), lambda b, p, t: (b, t, p)),
        out_shape=jax.ShapeDtypeStruct((B, S, WBQ), jnp.bfloat16),
        compiler_params=pltpu.CompilerParams(
            dimension_semantics=("parallel", "parallel", "arbitrary"), vmem_limit_bytes=VMEM_LIMIT),
        name="attn_b",
    )(qb, kb, vb, bias, sink)


def _split3(x):
    bf = jnp.bfloat16
    p1 = x.astype(bf).astype(jnp.float32)
    r1 = x - p1
    p2 = r1.astype(bf).astype(jnp.float32)
    return p1, p2, (r1 - p2).astype(bf).astype(jnp.float32)


def _mlstm_chain(q2, k2, vext, gt, tri, eye, state, mrow, lo, out):
    f32 = jnp.float32
    bf = jnp.bfloat16
    lf = jnp.minimum(gt, 0.0) - jnp.log(1.0 + jnp.exp(-jnp.abs(gt)))
    lf3 = _split3(lf)
    b_row = sum(lax.dot_general(p.astype(bf), tri, _NT, preferred_element_type=f32) for p in lf3)
    yield
    b_tot = jnp.sum(lf, axis=1, keepdims=True)
    visible = tri > 0
    rowsel = lax.broadcasted_iota(jnp.int32, (LANES, LC), 0) < HEAD_DIM
    zero = jnp.zeros_like(q2)
    per_head = []
    for hh in (0, 1):
        br = b_row[2 + hh:3 + hh, :]
        ir = gt[hh:hh + 1, :]
        bt = b_tot[2 + hh:3 + hh, :]
        mp = mrow[hh:hh + 1, :]
        bc = sum(lax.dot_general(tri, jnp.broadcast_to(p[2 + hh:3 + hh, :], (LANES, LC)).astype(bf), _NT,
                                 preferred_element_type=f32) for p in lf3)
        yield
        dmat = jnp.where(visible, bc - br + ir, NEG)
        m_in = bc + mp
        m_t = jnp.maximum(jnp.max(dmat, axis=1, keepdims=True), m_in)
        qm = jnp.where(lo if hh == 0 else jnp.logical_not(lo), q2, zero)
        qk = lax.dot_general(qm, k2, _NT, preferred_element_type=f32)
        yield
        p = (jnp.exp(dmat - m_t) * qk).astype(bf)
        r = jnp.dot(p, vext, preferred_element_type=f32)
        yield
        a_t = jnp.exp(m_in - m_t)
        m_loc = jnp.max(bt - br + ir, axis=1, keepdims=True)
        m_new = jnp.maximum(bt + mp, m_loc)
        decay = jnp.exp(bt + mp - m_new)
        per_head.append((r, a_t, m_t, bt - m_new, decay, m_new, ir - br))
    (r0, a0, mt0, s0, d0, mn0, g0), (r1, a1, mt1, s1, d1, mn1, g1) = per_head

    inter = jnp.dot(q2, state.astype(bf), preferred_element_type=f32)
    yield
    a2 = jnp.where(lo, a0, a1)
    mt2 = jnp.where(lo, mt0, mt1)
    num = jnp.where(lo, r0[:, :LANES], r1[:, :LANES]) + a2 * inter[:, :LANES]
    den = jnp.where(lo, r0[:, LANES:], r1[:, LANES:]) + a2 * inter[:, LANES:]
    h = num / jnp.maximum(jnp.abs(den), jnp.exp(-mt2))

    g3 = (_split3(g0), _split3(g1))
    gcol = sum(lax.dot_general(eye, jnp.where(rowsel, pa, pb).astype(bf), _NT, preferred_element_type=f32)
               for pa, pb in zip(*g3))
    yield
    lo1 = lo[0:1, :]
    e2 = jnp.exp(gcol + jnp.where(lo1, s0, s1))
    ke = (k2.astype(f32) * e2).astype(bf)
    s_loc = lax.dot_general(ke, vext, _TN, preferred_element_type=f32)
    yield
    row = lax.broadcasted_iota(jnp.int32, (LANES, 2 * LANES), 0)
    col = lax.broadcasted_iota(jnp.int32, (LANES, 2 * LANES), 1)
    same_head = (row < HEAD_DIM) == ((col % LANES) < HEAD_DIM)
    dec_rows = jnp.where(row < HEAD_DIM, d0[:, 0:1], d1[:, 0:1])
    new_state = dec_rows * state + jnp.where(same_head, s_loc, 0.0)
    out.append((h, new_state, mn0, mn1))


def _mlstm_kernel(fqp, fqm, fqn, fkp, fkm, fkn, fv, bqp, bqm, bqn, bkp, bkm, bkn, bv,
                  gtf_ref, gtb_ref, gbias_ref, cq_ref, ck_ref, tri_ref, hf_ref, hb_ref,
                  x_scr, st_scr, m_scr, *, nc):
    c = pl.program_id(1)
    f32 = jnp.float32
    bf = jnp.bfloat16
    lo = lax.broadcasted_iota(jnp.int32, (LC, LANES), 1) < HEAD_DIM
    eye = (lax.broadcasted_iota(jnp.int32, (LC, LC), 0) == lax.broadcasted_iota(jnp.int32, (LC, LC), 1)).astype(bf)

    @pl.when(c == 0)
    def _():
        st_scr[...] = jnp.zeros_like(st_scr)
        m_scr[...] = jnp.zeros_like(m_scr)

    dirs = ((fqp, fqm, fqn, fkp, fkm, fkn, fv, gtf_ref, hf_ref, c),
            (bqp, bqm, bqn, bkp, bkm, bkn, bv, gtb_ref, hb_ref, nc - 1 - c))
    chains = []
    for drn, (qp, qm, qn, kp, km, kn, v_ref, gt_ref, h_ref, ce) in enumerate(dirs):
        keep_prev = (ce > 0).astype(f32)
        keep_next = (ce < nc - 1).astype(f32)

        def conv_silu(t, p_ref, m_ref, n_ref, w_ref):
            x_scr[drn, t, 0:HALO, :] = p_ref[...].astype(f32) * keep_prev
            x_scr[drn, t, HALO:HALO + LC, :] = m_ref[...].astype(f32)
            x_scr[drn, t, HALO + LC:, :] = n_ref[...].astype(f32) * keep_next
            acc = jnp.zeros((LC, WC), f32)
            for j in range(CONV_WIDTH):
                acc = acc + x_scr[drn, t, pl.ds(HALO - CONV_WIDTH // 2 + j, LC), :] * w_ref[j:j + 1, :]
            return acc * jax.nn.sigmoid(acc)

        q_all = conv_silu(0, qp, qm, qn, cq_ref).astype(bf)
        k_all = (conv_silu(1, kp, km, kn, ck_ref) * (HEAD_DIM ** -0.5)).astype(bf)
        tri = tri_ref[drn]
        for pr in range(HEADS_C // 2):
            sl = slice(pr * LANES, (pr + 1) * LANES)
            v2 = v_ref[:, sl]
            vext = jnp.concatenate([v2, jnp.ones_like(v2)], axis=1)
            gt = gt_ref[pr] + gbias_ref[pr, drn]
            out = []
            gen = _mlstm_chain(q_all[:, sl], k_all[:, sl], vext, gt, tri, eye,
                               st_scr[drn, pr], m_scr[drn, pr], lo, out)
            chains.append((gen, out, drn, pr, sl, h_ref))

    live = list(chains)
    while live:
        live = [ch for ch in live if next(ch[0], StopIteration) is not StopIteration]
    for _, out, drn, pr, sl, h_ref in chains:
        h, new_state, mn0, mn1 = out[0]
        st_scr[drn, pr] = new_state
        m_scr[drn, pr, 0:1, :] = mn0
        m_scr[drn, pr, 1:2, :] = mn1
        h_ref[:, sl] = h.astype(h_ref.dtype)


def _mlstm(qc, kc, vc, gt, gate_b, conv_q, conv_k, tri):
    B, S, _ = qc.shape
    nc = S // LC
    npair = HEADS_C // 2
    hpc = LC // HALO
    nh = S // HALO

    def pos(drn, c):
        return c if drn == 0 else nc - 1 - c

    def cur(drn):
        return pl.BlockSpec((None, LC, WC), lambda b, c: (b, pos(drn, c), 0))

    def prev(drn):
        return pl.BlockSpec((None, HALO, WC), lambda b, c: (b, jnp.maximum(pos(drn, c) * hpc - 1, 0), 0))

    def nxt(drn):
        return pl.BlockSpec((None, HALO, WC), lambda b, c: (b, jnp.minimum((pos(drn, c) + 1) * hpc, nh - 1), 0))

    def gates(drn):
        return pl.BlockSpec((None, npair, None, 8, LC), lambda b, c: (b, 0, drn, 0, pos(drn, c)))

    def const(shape):
        return pl.BlockSpec(shape, lambda b, c: (0,) * len(shape))

    in_specs, args = [], []
    for drn in (0, 1):
        in_specs += [prev(drn), cur(drn), nxt(drn), prev(drn), cur(drn), nxt(drn), cur(drn)]
        args += [qc, qc, qc, kc, kc, kc, vc]
    in_specs += [gates(0), gates(1), const(gate_b.shape), const(conv_q.shape), const(conv_k.shape), const(tri.shape)]
    args += [gt, gt, gate_b, conv_q, conv_k, tri]
    h_shape = jax.ShapeDtypeStruct((B, S, WC), jnp.bfloat16)
    return pl.pallas_call(
        functools.partial(_mlstm_kernel, nc=nc),
        grid=(B, nc),
        in_specs=in_specs,
        out_specs=(cur(0), cur(1)),
        out_shape=(h_shape, h_shape),
        scratch_shapes=[
            pltpu.VMEM((2, 2, LC + 2 * HALO, WC), jnp.float32),
            pltpu.VMEM((2, npair, LANES, 2 * LANES), jnp.float32),
            pltpu.VMEM((2, npair, 8, LANES), jnp.float32),
        ],
        compiler_params=pltpu.CompilerParams(
            dimension_semantics=("parallel", "arbitrary"), vmem_limit_bytes=VMEM_LIMIT),
        name="mlstm",
    )(*args)


def _post_kernel(x_ref, ya_ref, yb_ref, hf_ref, hb_ref, oc_ref, p_ref, ng_ref, hm_ref, wo_ref, fg_ref,
                 wup_ref, wdn_ref, pp_ref, pg_ref, fin_ref, out_ref, *, final):
    f32 = jnp.float32
    bf = jnp.bfloat16
    hs = hf_ref[...].astype(f32) + hb_ref[...].astype(f32)
    sq = hs * hs
    sq_hi = sq.astype(bf)
    sq_lo = (sq - sq_hi.astype(f32)).astype(bf)
    ms = (jnp.dot(sq_hi, hm_ref[...], preferred_element_type=f32)
          + jnp.dot(sq_lo, hm_ref[...], preferred_element_type=f32))
    yc = (jax.nn.sigmoid(oc_ref[...].astype(f32)) * (hs * lax.rsqrt(ms + EPS) * ng_ref[...])).astype(bf)
    y = jnp.concatenate([ya_ref[...], yb_ref[...], yc], axis=1)
    x1 = x_ref[...] + jnp.dot(y, wo_ref[...], preferred_element_type=f32)
    h = _rms(x1, fg_ref[...]).astype(bf)
    ffn = None
    for j in range(D_FF // FF_CHUNK):
        gate = jnp.dot(h, wup_ref[:, j * FF_CHUNK:(j + 1) * FF_CHUNK], preferred_element_type=f32)
        up = jnp.dot(h, wup_ref[:, D_FF + j * FF_CHUNK:D_FF + (j + 1) * FF_CHUNK], preferred_element_type=f32)
        act = (gate * jax.nn.sigmoid(gate) * up).astype(bf)
        part = jnp.dot(act, wdn_ref[j * FF_CHUNK:(j + 1) * FF_CHUNK, :], preferred_element_type=f32)
        ffn = part if ffn is None else ffn + part
    x2 = x1 + ffn
    emb = jnp.dot(p_ref[...].astype(bf), pp_ref[...], preferred_element_type=f32)
    gate = jax.nn.sigmoid(jnp.dot(x2.astype(bf), pg_ref[...], preferred_element_type=f32))
    x3 = x2 + emb * gate
    if final:
        x3 = _rms(x3, fin_ref[...])
    out_ref[...] = x3


def _post(x, ya, yb, hf, hb, oc, p, ng, hm, wo, fg, wup, wdn, pp, pg, fin, final):
    B, S, D = x.shape
    tm = TM_POST

    def tok(width):
        return pl.BlockSpec((None, tm, width), lambda b, i: (b, i, 0))

    def const(shape):
        return pl.BlockSpec(shape, lambda b, i: (0,) * len(shape), pipeline_mode=pl.Buffered(1))

    return pl.pallas_call(
        functools.partial(_post_kernel, final=final),
        grid=(B, S // tm),
        in_specs=[
            tok(D), tok(WA), tok(WBQ), tok(WC), tok(WC), tok(WC), tok(p.shape[-1]),
            const(ng.shape), const(hm.shape), const(wo.shape), const((1, D)),
            const(wup.shape), const(wdn.shape), const(pp.shape), const(pg.shape), const((1, D)),
        ],
        out_specs=tok(D),
        out_shape=jax.ShapeDtypeStruct((B, S, D), jnp.float32),
        compiler_params=pltpu.CompilerParams(
            dimension_semantics=("parallel", "parallel"), vmem_limit_bytes=VMEM_LIMIT),
        name="post",
    )(x, ya, yb, hf, hb, oc, p, ng, hm, wo, fg, wup, wdn, pp, pg, fin)


def _t5_bucket(rel):
    half_b = N_BUCKETS // 2
    max_exact = half_b // 2
    n = jnp.abs(rel)
    nf = jnp.maximum(n, 1).astype(jnp.float32)
    large = max_exact + (jnp.log(nf / max_exact) / jnp.log(jnp.float32(REL_MAX_DIST / max_exact))
                         * (half_b - max_exact)).astype(jnp.int32)
    large = jnp.minimum(large, half_b - 1)
    return jnp.where(rel > 0, half_b, 0) + jnp.where(n < max_exact, n, large)


def _band_bias(table, half, dil):
    W = TQ + 2 * half
    iq = jnp.arange(TQ, dtype=jnp.int32)[:, None]
    ik = jnp.arange(W, dtype=jnp.int32)[None, :]
    tiles = []
    for shift in (0, half, 2 * half):
        rel = ik - iq - shift
        vals = jnp.transpose(table[_t5_bucket(rel * dil)], (2, 0, 1)).astype(jnp.float32)
        tiles.append(jnp.where((jnp.abs(rel) <= half)[None], vals, NEG))
    return jnp.stack(tiles, 0)


_B_ORDER = (0, 2, 1, 3)


def _perm_heads(w, order, axis):
    blocks = [lax.slice_in_dim(w, h * HEAD_DIM, (h + 1) * HEAD_DIM, axis=axis) for h in order]
    return jnp.concatenate(blocks, axis=axis)


def kernel(x, p, rel_bias, attn_norm, w_in, qk_conv, gate_bias, sink_logits, mlstm_norm, w_out, ffn_norm,
           w_up, w_down, ple_proj, ple_gate, final_norm):
    B, S, D = x.shape
    depth = w_in.shape[0]
    assert D == D_MODEL and S % TB_A == 0 and S // 16 >= TQ + 2 * HALF_A
    f32 = jnp.float32
    bf = jnp.bfloat16

    bias_a = jnp.stack([_band_bias(rel_bias[:, :HEADS_A], HALF_A, d) for d in DILATIONS], 0)
    bias_b = _band_bias(rel_bias[:, HEADS_A:], HALF_B, 1)
    bias_b = jnp.stack([bias_b[:, h] for h in _B_ORDER], 1).reshape(3, 2, 2, TQ, TQ + 2 * HALF_B)
    tri_lo = (jnp.arange(LC)[:, None] >= jnp.arange(LC)[None, :]).astype(f32)
    tri = jnp.stack([tri_lo, tri_lo.T], 0).astype(bf)
    head_id = jnp.arange(WC) // HEAD_DIM
    head_mean = ((head_id[:, None] == head_id[None, :]).astype(f32) / HEAD_DIM).astype(bf)

    gate_off = D_MAIN
    for i in range(depth):
        w = w_in[i]
        qb_off = 3 * WA
        w_main = jnp.concatenate([
            w[:, :qb_off], _perm_heads(w[:, qb_off:qb_off + WBQ], _B_ORDER, 1), w[:, qb_off + WBQ:gate_off]],
            axis=1).astype(bf)
        wg = w[:, gate_off:]
        gb = gate_bias[i]
        rows, brow = [], []
        zrow = jnp.zeros((4, D), f32)
        for pr in range(HEADS_C // 2):
            for drn in range(2):
                idx = [(2 * drn) * HEADS_C + 2 * pr, (2 * drn) * HEADS_C + 2 * pr + 1,
                       (2 * drn + 1) * HEADS_C + 2 * pr, (2 * drn + 1) * HEADS_C + 2 * pr + 1]
                rows += [wg[:, idx].T, zrow]
                brow += [gb[jnp.array(idx)], jnp.zeros((4,), f32)]
        wg_t = jnp.concatenate(rows, 0).astype(bf)
        gate_b = jnp.broadcast_to(jnp.concatenate(brow).reshape(3, 2, 8, 1), (3, 2, 8, LC))
        conv = jnp.concatenate([qk_conv[i], jnp.zeros((8 - CONV_WIDTH, 2 * WC), f32)], 0)
        conv_q, conv_k = conv[:, :WC], conv[:, WC:]
        sink = jnp.broadcast_to(sink_logits[i][jnp.array(_B_ORDER)].reshape(2, 2, 1, 1), (2, 2, 8, LANES))
        wo = w_out[i]
        wo = jnp.concatenate([wo[:WA], _perm_heads(wo[WA:WA + WBQ], _B_ORDER, 0), wo[WA + WBQ:]], axis=0).astype(bf)

        a1, a4, a16, qb, kb, vb, qc, kc, vc, oc, gt = _inproj(x, attn_norm[i][None], w_main, wg_t)
        ya = _attn_a(a1.reshape(B, 1, S, 3 * WA), a4, a16, bias_a)
        yb = _attn_b(qb, kb, vb, bias_b, sink)
        hf, hb = _mlstm(qc, kc, vc, gt, gate_b, conv_q, conv_k, tri)
        x = _post(x, ya, yb, hf, hb, oc, p[i], mlstm_norm[i][None], head_mean, wo, ffn_norm[i][None], w_up[i].astype(bf), w_down[i].astype(bf),
                  ple_proj[i].astype(bf), ple_gate[i].astype(bf), final_norm[None], final=(i == depth - 1))
    return x
```

```python
import functools

import jax
import jax.numpy as jnp
from jax import lax
from jax.experimental import pallas as pl
from jax.experimental.pallas import tpu as pltpu

D_MODEL = 1024
HEAD_DIM = 64
HEADS_A = 6
HEADS_B = 4
KV_HEADS_B = 2
HEADS_C = 6
DILATIONS = (1, 4, 16)
HALF_A = 64
HALF_B = 128
CONV_WIDTH = 5
N_BUCKETS = 32
REL_MAX_DIST = 1024
D_FF = 2816
EPS = 1e-6
NEG = -1e30
WA = HEADS_A * HEAD_DIM
WBQ = HEADS_B * HEAD_DIM
WBK = KV_HEADS_B * HEAD_DIM
WC = HEADS_C * HEAD_DIM
D_MAIN = 3 * WA + WBQ + 2 * WBK + 4 * WC

LANES = 128
BF16_ROWS = 16
VMEM_LIMIT = 56 * 1024 * 1024

TM_IN = 512
RC_IN = 256
TM_POST = 512
FF_CHUNK = 256
TQ = 128
TB_A = 2048
TB_B = 1024
TILES_A = 8
TILES_B = 4
LC = 128
HALO = BF16_ROWS

_NT = (((1,), (1,)), ((), ()))
_TN = (((0,), (0,)), ((), ()))
_HI = lax.Precision.HIGHEST


def _rms(x, g):
    return x * lax.rsqrt(jnp.mean(x * x, axis=-1, keepdims=True) + EPS) * g


def _inproj_kernel(x_ref, xp_ref, xn_ref, g_ref, w_ref, wg_ref, cw_ref,
                   a1_ref, a4_ref, a16_ref, qb_ref, kb_ref, vb_ref,
                   qc_ref, kc_ref, vc_ref, oc_ref, gt_ref, z_scr, c_scr):
    tm = x_ref.shape[0]
    i = pl.program_id(1)
    bf = jnp.bfloat16
    f32 = jnp.float32
    rc = RC_IN
    nl = 3 * WA // LANES
    b_lo = 3 * WA
    qk_lo = b_lo + WBQ + 2 * WBK
    vo_lo = qk_lo + 2 * WC

    hh = _rms(jnp.concatenate([xp_ref[...], xn_ref[...]], axis=0), g_ref[...]).astype(bf)
    zh = jnp.dot(hh, w_ref[:, qk_lo:vo_lo], preferred_element_type=f32)
    c_scr[0:HALO, :] = zh[0:HALO] * (i > 0).astype(f32)
    c_scr[HALO + tm:, :] = zh[HALO:] * (i < pl.num_programs(1) - 1).astype(f32)

    def conv_rows(r0):
        for s0 in range(r0, r0 + rc, 128):
            acc = jnp.zeros((128, 2 * WC), f32)
            for j in range(CONV_WIDTH):
                acc = acc + c_scr[pl.ds(s0 + HALO - CONV_WIDTH // 2 + j, 128), :] * cw_ref[j:j + 1, :]
            act = acc * jax.nn.sigmoid(acc)
            qc_ref[s0:s0 + 128, :] = act[:, :WC].astype(bf)
            kc_ref[s0:s0 + 128, :] = (act[:, WC:] * (HEAD_DIM ** -0.5)).astype(bf)

    for r0 in range(0, tm, rc):
        rows = slice(r0, r0 + rc)
        h = _rms(x_ref[rows, :], g_ref[...]).astype(bf)
        za = jnp.dot(h, w_ref[:, 0:b_lo], preferred_element_type=f32)
        a1_ref[rows, :] = za.astype(bf)
        for j in range(nl):
            z_scr[j, rows, :] = za[:, j * LANES:(j + 1) * LANES]
        for d, ref in ((4, a4_ref), (16, a16_ref)):
            for c in range(d):
                for j in range(nl):
                    ref[c, r0 // d:(r0 + rc) // d, j * LANES:(j + 1) * LANES] = \
                        z_scr[j, pl.ds(r0 + c, rc // d, stride=d), :].astype(bf)
        zb = jnp.dot(h, w_ref[:, b_lo:qk_lo], preferred_element_type=f32)
        qb_ref[rows, :] = zb[:, :WBQ].astype(bf)
        kb_ref[rows, :] = zb[:, WBQ:WBQ + WBK].astype(bf)
        vb_ref[rows, :] = zb[:, WBQ + WBK:].astype(bf)
        c_scr[HALO + r0:HALO + r0 + rc, :] = jnp.dot(h, w_ref[:, qk_lo:vo_lo], preferred_element_type=f32)
        zv = jnp.dot(h, w_ref[:, vo_lo:], preferred_element_type=f32)
        vc_ref[rows, :] = zv[:, :WC].astype(bf)
        oc_ref[rows, :] = zv[:, WC:].astype(bf)
        gt = lax.dot_general(wg_ref[...], h, _NT, preferred_element_type=f32)
        gt_ref[:, :, :, rows] = gt.reshape(gt_ref.shape[:-1] + (rc,))
        if r0 > 0:
            conv_rows(r0 - rc)
    conv_rows(tm - rc)


def _inproj(x, g, w_main, wg_t, conv_w):
    B, S, D = x.shape
    tm = TM_IN
    nt = S // tm
    hpb = tm // HALO
    nh = S // HALO
    bf = jnp.bfloat16
    f32 = jnp.float32

    def tok(width):
        return pl.BlockSpec((None, tm, width), lambda b, i: (b, i, 0))

    def cls(d):
        return pl.BlockSpec((None, d, tm // d, 3 * WA), lambda b, i: (b, 0, i, 0))

    out_shape = (
        jax.ShapeDtypeStruct((B, S, 3 * WA), bf),
        jax.ShapeDtypeStruct((B, 4, S // 4, 3 * WA), bf),
        jax.ShapeDtypeStruct((B, 16, S // 16, 3 * WA), bf),
        jax.ShapeDtypeStruct((B, S, WBQ), bf),
        jax.ShapeDtypeStruct((B, S, WBK), bf),
        jax.ShapeDtypeStruct((B, S, WBK), bf),
        jax.ShapeDtypeStruct((B, S, WC), bf),
        jax.ShapeDtypeStruct((B, S, WC), bf),
        jax.ShapeDtypeStruct((B, S, WC), bf),
        jax.ShapeDtypeStruct((B, S, WC), bf),
        jax.ShapeDtypeStruct((B, 3, 2, 8, S), f32),
    )
    out_specs = (
        tok(3 * WA), cls(4), cls(16), tok(WBQ), tok(WBK), tok(WBK),
        tok(WC), tok(WC), tok(WC), tok(WC),
        pl.BlockSpec((None, 3, 2, 8, tm), lambda b, i: (b, 0, 0, 0, i)),
    )
    return pl.pallas_call(
        _inproj_kernel,
        grid=(B, nt),
        in_specs=[
            pl.BlockSpec((None, tm, D), lambda b, i: (b, i, 0)),
            pl.BlockSpec((None, HALO, D), lambda b, i: (b, jnp.maximum(i * hpb - 1, 0), 0)),
            pl.BlockSpec((None, HALO, D), lambda b, i: (b, jnp.minimum((i + 1) * hpb, nh - 1), 0)),
            pl.BlockSpec((1, D), lambda b, i: (0, 0)),
            pl.BlockSpec((D, D_MAIN), lambda b, i: (0, 0)),
            pl.BlockSpec((48, D), lambda b, i: (0, 0)),
            pl.BlockSpec(conv_w.shape, lambda b, i: (0, 0)),
        ],
        out_specs=out_specs,
        out_shape=out_shape,
        scratch_shapes=[pltpu.VMEM((3 * WA // LANES, tm, LANES), f32),
                        pltpu.VMEM((tm + 2 * HALO, 2 * WC), f32)],
        compiler_params=pltpu.CompilerParams(
            dimension_semantics=("parallel", "parallel"), vmem_limit_bytes=VMEM_LIMIT),
        name="inproj",
    )(x, x, x, g, w_main, wg_t, conv_w)


def _band_tile(q2, kw, vw, bias0, bias1, lo, out, sinks=None):
    q2 = q2 * jnp.asarray(HEAD_DIM ** -0.5, q2.dtype)
    vext = jnp.concatenate([vw, jnp.ones_like(vw)], axis=1)
    zero = jnp.zeros_like(q2)
    for hh, bias in ((0, bias0), (1, bias1)):
        qm = jnp.where(lo if hh == 0 else jnp.logical_not(lo), q2, zero)
        s = lax.dot_general(qm, kw, _NT, preferred_element_type=jnp.float32)
        yield
        s = s + bias
        m = jnp.max(s, axis=1, keepdims=True)
        if sinks is not None:
            m = jnp.maximum(m, sinks[hh])
        r = jnp.dot(jnp.exp(s - m).astype(jnp.bfloat16), vext, preferred_element_type=jnp.float32)
        yield
        out.append((m, r))


def _round_robin(gens):
    live = list(gens)
    while live:
        live = [g for g in live if next(g, StopIteration) is not StopIteration]


def _attn_a_kernel(q1_ref, q4_ref, q16_ref, k1_ref, v1_ref, k4_ref, v4_ref, k16_ref, v16_ref,
                   bias_ref, o_ref, acc_scr, m_scr, l_scr, *, seq):
    tb = o_ref.shape[0]
    tbi = pl.program_id(2)
    lo = lax.broadcasted_iota(jnp.int32, (TQ, LANES), 1) < HEAD_DIM
    W = TQ + 2 * HALF_A
    groups = ((1, q1_ref, k1_ref, v1_ref), (4, q4_ref, k4_ref, v4_ref), (16, q16_ref, k16_ref, v16_ref))
    for g, (d, q_ref, k_ref, v_ref) in enumerate(groups):
        L = seq // d
        nsub = tb // d
        ntile = nsub // TQ

        def tiles(it, carry, g=g, d=d, q_ref=q_ref, k_ref=k_ref, v_ref=v_ref, L=L, nsub=nsub, ntile=ntile):
            work = []
            for u in range(TILES_A):
                idx = it * TILES_A + u
                c = idx // ntile
                jj = idx % ntile
                sub0 = tbi * nsub + jj * TQ
                ws = pl.multiple_of(jnp.clip(sub0 - HALF_A, 0, L - W), HALF_A)
                var = (sub0 - ws) // HALF_A
                q2 = q_ref[c, pl.ds(pl.multiple_of(jj * TQ, TQ), TQ), :]
                kw = k_ref[c, pl.ds(ws, W), :]
                vw = v_ref[c, pl.ds(ws, W), :]
                out = []
                rows = pl.ds(jj * (TQ * d) + c, TQ, stride=d) if d > 1 else pl.ds(pl.multiple_of(jj * TQ, TQ), TQ)
                work.append((_band_tile(q2, kw, vw, bias_ref[g, var, 0], bias_ref[g, var, 1], lo, out), out, rows))
            _round_robin([w[0] for w in work])
            for _, ((m0, r0), (m1, r1)), rows in work:
                acc_scr[g, rows, :] = jnp.where(lo, r0[:, :LANES], r1[:, :LANES])
                l_scr[g, rows, :] = jnp.where(lo, r0[:, LANES:], r1[:, LANES:])
                m_scr[g, rows, :] = jnp.where(lo, m0, m1)
            return carry

        lax.fori_loop(0, d * ntile // TILES_A, tiles, 0)

    rc = 256

    def combine(i, carry):
        rows = pl.ds(pl.multiple_of(i * rc, rc), rc)
        m0, m1, m2 = m_scr[0, rows, :], m_scr[1, rows, :], m_scr[2, rows, :]
        mm = jnp.maximum(jnp.maximum(m0, m1), m2)
        e0, e1, e2 = jnp.exp(m0 - mm), jnp.exp(m1 - mm), jnp.exp(m2 - mm)
        num = e0 * acc_scr[0, rows, :] + e1 * acc_scr[1, rows, :] + e2 * acc_scr[2, rows, :]
        den = e0 * l_scr[0, rows, :] + e1 * l_scr[1, rows, :] + e2 * l_scr[2, rows, :]
        o_ref[rows, :] = (num / den).astype(o_ref.dtype)
        return carry

    lax.fori_loop(0, tb // rc, combine, 0)


def _attn_a(a1, a4, a16, bias):
    B, _, S, _ = a1.shape
    tb = TB_A
    npair = HEADS_A // 2

    def qspec(d):
        return pl.BlockSpec((None, d, tb // d, LANES), lambda b, p, t: (b, 0, t, p))

    def kvspec(d, off):
        return pl.BlockSpec((None, d, S // d, LANES), lambda b, p, t, off=off: (b, 0, 0, off + p))

    in_specs = [qspec(1), qspec(4), qspec(16)]
    args = [a1, a4, a16]
    for d, arr in ((1, a1), (4, a4), (16, a16)):
        in_specs += [kvspec(d, npair), kvspec(d, 2 * npair)]
        args += [arr, arr]
    in_specs.append(pl.BlockSpec((3, 3, 2, TQ, TQ + 2 * HALF_A), lambda b, p, t: (0, 0, p, 0, 0)))
    args.append(bias)
    return pl.pallas_call(
        functools.partial(_attn_a_kernel, seq=S),
        grid=(B, npair, S // tb),
        in_specs=in_specs,
        out_specs=pl.BlockSpec((None, tb, LANES), lambda b, p, t: (b, t, p)),
        out_shape=jax.ShapeDtypeStruct((B, S, WA), jnp.bfloat16),
        scratch_shapes=[pltpu.VMEM((3, tb, LANES), jnp.float32)] * 3,
        compiler_params=pltpu.CompilerParams(
            dimension_semantics=("parallel", "parallel", "arbitrary"), vmem_limit_bytes=VMEM_LIMIT),
        name="attn_a",
    )(*args)


def _attn_b_kernel(q_ref, k_ref, v_ref, bias_ref, sink_ref, o_ref, *, seq):
    tb = o_ref.shape[0]
    tbi = pl.program_id(2)
    lo = lax.broadcasted_iota(jnp.int32, (TQ, LANES), 1) < HEAD_DIM
    W = TQ + 2 * HALF_B
    s0 = sink_ref[0, 0:1, 0:1]
    s1 = sink_ref[1, 0:1, 0:1]

    def tiles(it, carry):
        work = []
        for u in range(TILES_B):
            jj = it * TILES_B + u
            sub0 = tbi * tb + jj * TQ
            ws = pl.multiple_of(jnp.clip(sub0 - HALF_B, 0, seq - W), HALF_B)
            var = (sub0 - ws) // HALF_B
            rows = pl.ds(pl.multiple_of(jj * TQ, TQ), TQ)
            out = []
            work.append((_band_tile(q_ref[rows, :], k_ref[pl.ds(ws, W), :], v_ref[pl.ds(ws, W), :],
                                    bias_ref[var, 0], bias_ref[var, 1], lo, out, sinks=(s0, s1)), out, rows))
        _round_robin([w[0] for w in work])
        for _, ((m0, r0), (m1, r1)), rows in work:
            num = jnp.where(lo, r0[:, :LANES], r1[:, :LANES])
            den = jnp.where(lo, r0[:, LANES:] + jnp.exp(s0 - m0), r1[:, LANES:] + jnp.exp(s1 - m1))
            o_ref[rows, :] = (num / den).astype(o_ref.dtype)
        return carry

    lax.fori_loop(0, tb // TQ // TILES_B, tiles, 0)


def _attn_b(qb, kb, vb, bias, sink):
    B, S, _ = qb.shape
    tb = TB_B
    npair = HEADS_B // 2
    return pl.pallas_call(
        functools.partial(_attn_b_kernel, seq=S),
        grid=(B, npair, S // tb),
        in_specs=[
            pl.BlockSpec((None, tb, LANES), lambda b, p, t: (b, t, p)),
            pl.BlockSpec((None, S, LANES), lambda b, p, t: (b, 0, 0)),
            pl.BlockSpec((None, S, LANES), lambda b, p, t: (b, 0, 0)),
            pl.BlockSpec((3, None, 2, TQ, TQ + 2 * HALF_B), lambda b, p, t: (0, p, 0, 0, 0)),
            pl.BlockSpec((None, 2, 8, LANES), lambda b, p, t: (p, 0, 0, 0)),
        ],
        out_specs=pl.BlockSpec((None, tb, LANES), lambda b, p, t: (b, t, p)),
        out_shape=jax.ShapeDtypeStruct((B, S, WBQ), jnp.bfloat16),
        compiler_params=pltpu.CompilerParams(
            dimension_semantics=("parallel", "parallel", "arbitrary"), vmem_limit_bytes=VMEM_LIMIT),
        name="attn_b",
    )(qb, kb, vb, bias, sink)


def _split3(x):
    bf = jnp.bfloat16
    p1 = x.astype(bf).astype(jnp.float32)
    r1 = x - p1
    p2 = r1.astype(bf).astype(jnp.float32)
    return p1, p2, (r1 - p2).astype(bf).astype(jnp.float32)


def _mlstm_chain(q2, k2, vext, gt, tri, eye, state, mrow, lo, out):
    f32 = jnp.float32
    bf = jnp.bfloat16
    lf = jnp.minimum(gt, 0.0) - jnp.log(1.0 + jnp.exp(-jnp.abs(gt)))
    lf3 = _split3(lf)
    b_row = sum(lax.dot_general(p.astype(bf), tri, _NT, preferred_element_type=f32) for p in lf3)
    yield
    b_tot = jnp.sum(lf, axis=1, keepdims=True)
    visible = tri > 0
    rowsel = lax.broadcasted_iota(jnp.int32, (LANES, LC), 0) < HEAD_DIM
    zero = jnp.zeros_like(q2)
    per_head = []
    for hh in (0, 1):
        br = b_row[2 + hh:3 + hh, :]
        ir = gt[hh:hh + 1, :]
        bt = b_tot[2 + hh:3 + hh, :]
        mp = mrow[hh:hh + 1, :]
        bc = sum(lax.dot_general(tri, jnp.broadcast_to(p[2 + hh:3 + hh, :], (LANES, LC)).astype(bf), _NT,
                                 preferred_element_type=f32) for p in lf3)
        yield
        dmat = jnp.where(visible, bc - br + ir, NEG)
        m_in = bc + mp
        m_t = jnp.maximum(jnp.max(dmat, axis=1, keepdims=True), m_in)
        qm = jnp.where(lo if hh == 0 else jnp.logical_not(lo), q2, zero)
        qk = lax.dot_general(qm, k2, _NT, preferred_element_type=f32)
        yield
        p = (jnp.exp(dmat - m_t) * qk).astype(bf)
        r = jnp.dot(p, vext, preferred_element_type=f32)
        yield
        a_t = jnp.exp(m_in - m_t)
        m_loc = jnp.max(bt - br + ir, axis=1, keepdims=True)
        m_new = jnp.maximum(bt + mp, m_loc)
        decay = jnp.exp(bt + mp - m_new)
        per_head.append((r, a_t, m_t, bt - m_new, decay, m_new, ir - br))
    (r0, a0, mt0, s0, d0, mn0, g0), (r1, a1, mt1, s1, d1, mn1, g1) = per_head

    inter = jnp.dot(q2, state.astype(bf), preferred_element_type=f32)
    yield
    a2 = jnp.where(lo, a0, a1)
    mt2 = jnp.where(lo, mt0, mt1)
    num = jnp.where(lo, r0[:, :LANES], r1[:, :LANES]) + a2 * inter[:, :LANES]
    den = jnp.where(lo, r0[:, LANES:], r1[:, LANES:]) + a2 * inter[:, LANES:]
    h = num / jnp.maximum(jnp.abs(den), jnp.exp(-mt2))

    g3 = (_split3(g0), _split3(g1))
    gcol = sum(lax.dot_general(eye, jnp.where(rowsel, pa, pb).astype(bf), _NT, preferred_element_type=f32)
               for pa, pb in zip(*g3))
    yield
    lo1 = lo[0:1, :]
    e2 = jnp.exp(gcol + jnp.where(lo1, s0, s1))
    ke = (k2.astype(f32) * e2).astype(bf)
    s_loc = lax.dot_general(ke, vext, _TN, preferred_element_type=f32)
    yield
    row = lax.broadcasted_iota(jnp.int32, (LANES, 2 * LANES), 0)
    col = lax.broadcasted_iota(jnp.int32, (LANES, 2 * LANES), 1)
    same_head = (row < HEAD_DIM) == ((col % LANES) < HEAD_DIM)
    dec_rows = jnp.where(row < HEAD_DIM, d0[:, 0:1], d1[:, 0:1])
    new_state = dec_rows * state + jnp.where(same_head, s_loc, 0.0)
    out.append((h, new_state, mn0, mn1))


def _mlstm_kernel(fq, fk, fv, bq, bk, bv, gtf_ref, gtb_ref, gbias_ref, tri_ref, hf_ref, hb_ref,
                  st_scr, m_scr):
    c = pl.program_id(1)
    bf = jnp.bfloat16
    lo = lax.broadcasted_iota(jnp.int32, (LC, LANES), 1) < HEAD_DIM
    eye = (lax.broadcasted_iota(jnp.int32, (LC, LC), 0) == lax.broadcasted_iota(jnp.int32, (LC, LC), 1)).astype(bf)

    @pl.when(c == 0)
    def _():
        st_scr[...] = jnp.zeros_like(st_scr)
        m_scr[...] = jnp.zeros_like(m_scr)

    dirs = ((fq, fk, fv, gtf_ref, hf_ref), (bq, bk, bv, gtb_ref, hb_ref))
    chains = []
    for drn, (q_ref, k_ref, v_ref, gt_ref, h_ref) in enumerate(dirs):
        tri = tri_ref[drn]
        for pr in range(HEADS_C // 2):
            sl = slice(pr * LANES, (pr + 1) * LANES)
            v2 = v_ref[:, sl]
            vext = jnp.concatenate([v2, jnp.ones_like(v2)], axis=1)
            gt = gt_ref[pr] + gbias_ref[pr, drn]
            out = []
            gen = _mlstm_chain(q_ref[:, sl], k_ref[:, sl], vext, gt, tri, eye,
                               st_scr[drn, pr], m_scr[drn, pr], lo, out)
            chains.append((gen, out, drn, pr, sl, h_ref))

    _round_robin([ch[0] for ch in chains])
    for _, out, drn, pr, sl, h_ref in chains:
        h, new_state, mn0, mn1 = out[0]
        st_scr[drn, pr] = new_state
        m_scr[drn, pr, 0:1, :] = mn0
        m_scr[drn, pr, 1:2, :] = mn1
        h_ref[:, sl] = h.astype(h_ref.dtype)


def _mlstm(qc, kc, vc, gt, gate_b, tri):
    B, S, _ = qc.shape
    nc = S // LC
    npair = HEADS_C // 2

    def pos(drn, c):
        return c if drn == 0 else nc - 1 - c

    def cur(drn):
        return pl.BlockSpec((None, LC, WC), lambda b, c: (b, pos(drn, c), 0))

    def gates(drn):
        return pl.BlockSpec((None, npair, None, 8, LC), lambda b, c: (b, 0, drn, 0, pos(drn, c)))

    def const(shape):
        return pl.BlockSpec(shape, lambda b, c: (0,) * len(shape))

    in_specs, args = [], []
    for drn in (0, 1):
        in_specs += [cur(drn), cur(drn), cur(drn)]
        args += [qc, kc, vc]
    in_specs += [gates(0), gates(1), const(gate_b.shape), const(tri.shape)]
    args += [gt, gt, gate_b, tri]
    h_shape = jax.ShapeDtypeStruct((B, S, WC), jnp.bfloat16)
    return pl.pallas_call(
        _mlstm_kernel,
        grid=(B, nc),
        in_specs=in_specs,
        out_specs=(cur(0), cur(1)),
        out_shape=(h_shape, h_shape),
        scratch_shapes=[
            pltpu.VMEM((2, npair, LANES, 2 * LANES), jnp.float32),
            pltpu.VMEM((2, npair, 8, LANES), jnp.float32),
        ],
        compiler_params=pltpu.CompilerParams(
            dimension_semantics=("parallel", "arbitrary"), vmem_limit_bytes=VMEM_LIMIT),
        name="mlstm",
    )(*args)


def _post_kernel(x_ref, ya_ref, yb_ref, hf_ref, hb_ref, oc_ref, p_ref, ng_ref, hm_ref, wo_ref, fg_ref,
                 wup_ref, wdn_ref, pp_ref, pg_ref, fin_ref, out_ref, *, final):
    f32 = jnp.float32
    bf = jnp.bfloat16
    hs = hf_ref[...].astype(f32) + hb_ref[...].astype(f32)
    sq = hs * hs
    sq_hi = sq.astype(bf)
    sq_lo = (sq - sq_hi.astype(f32)).astype(bf)
    ms = (jnp.dot(sq_hi, hm_ref[...], preferred_element_type=f32)
          + jnp.dot(sq_lo, hm_ref[...], preferred_element_type=f32))
    yc = (jax.nn.sigmoid(oc_ref[...].astype(f32)) * (hs * lax.rsqrt(ms + EPS) * ng_ref[...])).astype(bf)
    y = jnp.concatenate([ya_ref[...], yb_ref[...], yc], axis=1)
    x1 = x_ref[...] + jnp.dot(y, wo_ref[...], preferred_element_type=f32)
    h = _rms(x1, fg_ref[...]).astype(bf)
    ffn = None
    for j in range(D_FF // FF_CHUNK):
        gate = jnp.dot(h, wup_ref[:, j * FF_CHUNK:(j + 1) * FF_CHUNK], preferred_element_type=f32)
        up = jnp.dot(h, wup_ref[:, D_FF + j * FF_CHUNK:D_FF + (j + 1) * FF_CHUNK], preferred_element_type=f32)
        act = (gate * jax.nn.sigmoid(gate) * up).astype(bf)
        part = jnp.dot(act, wdn_ref[j * FF_CHUNK:(j + 1) * FF_CHUNK, :], preferred_element_type=f32)
        ffn = part if ffn is None else ffn + part
    x2 = x1 + ffn
    emb = jnp.dot(p_ref[...].astype(bf), pp_ref[...], preferred_element_type=f32)
    gate = jax.nn.sigmoid(jnp.dot(x2.astype(bf), pg_ref[...], preferred_element_type=f32))
    x3 = x2 + emb * gate
    if final:
        x3 = _rms(x3, fin_ref[...])
    out_ref[...] = x3


def _post(x, ya, yb, hf, hb, oc, p, layer, ng, hm, wo, fg, wup, wdn, pp, pg, fin, final):
    B, S, D = x.shape
    tm = TM_POST

    def tok(width):
        return pl.BlockSpec((None, tm, width), lambda b, i: (b, i, 0))

    def const(shape):
        return pl.BlockSpec(shape, lambda b, i: (0,) * len(shape), pipeline_mode=pl.Buffered(1))

    return pl.pallas_call(
        functools.partial(_post_kernel, final=final),
        grid=(B, S // tm),
        in_specs=[
            tok(D), tok(WA), tok(WBQ), tok(WC), tok(WC), tok(WC),
            pl.BlockSpec((None, None, tm, p.shape[-1]), lambda b, i: (layer, b, i, 0)),
            const(ng.shape), const(hm.shape), const(wo.shape), const((1, D)),
            const(wup.shape), const(wdn.shape), const(pp.shape), const(pg.shape), const((1, D)),
        ],
        out_specs=tok(D),
        out_shape=jax.ShapeDtypeStruct((B, S, D), jnp.float32),
        compiler_params=pltpu.CompilerParams(
            dimension_semantics=("parallel", "parallel"), vmem_limit_bytes=VMEM_LIMIT),
        name="post",
    )(x, ya, yb, hf, hb, oc, p, ng, hm, wo, fg, wup, wdn, pp, pg, fin)


def _t5_bucket(rel):
    half_b = N_BUCKETS // 2
    max_exact = half_b // 2
    n = jnp.abs(rel)
    nf = jnp.maximum(n, 1).astype(jnp.float32)
    large = max_exact + (jnp.log(nf / max_exact) / jnp.log(jnp.float32(REL_MAX_DIST / max_exact))
                         * (half_b - max_exact)).astype(jnp.int32)
    large = jnp.minimum(large, half_b - 1)
    return jnp.where(rel > 0, half_b, 0) + jnp.where(n < max_exact, n, large)


def _band_bias(table, half, dil):
    W = TQ + 2 * half
    H = table.shape[1]
    L = W + TQ
    j = jnp.arange(L, dtype=jnp.int32)
    off = jnp.where(j < W, j, j - L)
    tiles = []
    for shift in (0, half, 2 * half):
        rel = off - shift
        vals = jnp.where((jnp.abs(rel) <= half)[:, None], table[_t5_bucket(rel * dil)].astype(jnp.float32), NEG)
        row = jnp.broadcast_to(vals.T[:, None, :], (H, TQ, L)).reshape(H, TQ * L)
        tiles.append(row[:, :TQ * (L - 1)].reshape(H, TQ, L - 1)[:, :, :W])
    return jnp.stack(tiles, 0)


_B_ORDER = (0, 2, 1, 3)


def _perm_heads(w, order, axis):
    blocks = [lax.slice_in_dim(w, h * HEAD_DIM, (h + 1) * HEAD_DIM, axis=axis) for h in order]
    return jnp.concatenate(blocks, axis=axis)


def kernel(x, p, rel_bias, attn_norm, w_in, qk_conv, gate_bias, sink_logits, mlstm_norm, w_out, ffn_norm,
           w_up, w_down, ple_proj, ple_gate, final_norm):
    B, S, D = x.shape
    depth = w_in.shape[0]
    assert D == D_MODEL and S % TB_A == 0 and S // 16 >= TQ + 2 * HALF_A
    f32 = jnp.float32
    bf = jnp.bfloat16

    bias_a = jnp.stack([_band_bias(rel_bias[:, :HEADS_A], HALF_A, d) for d in DILATIONS], 0)
    bias_b = _band_bias(rel_bias[:, HEADS_A:], HALF_B, 1)
    bias_b = jnp.stack([bias_b[:, h] for h in _B_ORDER], 1).reshape(3, 2, 2, TQ, TQ + 2 * HALF_B)
    tri_lo = (jnp.arange(LC)[:, None] >= jnp.arange(LC)[None, :]).astype(f32)
    tri = jnp.stack([tri_lo, tri_lo.T], 0).astype(bf)
    head_id = jnp.arange(WC) // HEAD_DIM
    head_mean = ((head_id[:, None] == head_id[None, :]).astype(f32) / HEAD_DIM).astype(bf)

    gate_off = D_MAIN
    for i in range(depth):
        w = w_in[i]
        qb_off = 3 * WA
        w_main = jnp.concatenate([
            w[:, :qb_off], _perm_heads(w[:, qb_off:qb_off + WBQ], _B_ORDER, 1), w[:, qb_off + WBQ:gate_off]],
            axis=1).astype(bf)
        wg = w[:, gate_off:]
        gb = gate_bias[i]
        rows, brow = [], []
        zrow = jnp.zeros((4, D), f32)
        for pr in range(HEADS_C // 2):
            for drn in range(2):
                idx = [(2 * drn) * HEADS_C + 2 * pr, (2 * drn) * HEADS_C + 2 * pr + 1,
                       (2 * drn + 1) * HEADS_C + 2 * pr, (2 * drn + 1) * HEADS_C + 2 * pr + 1]
                rows += [wg[:, idx].T, zrow]
                brow += [gb[jnp.array(idx)], jnp.zeros((4,), f32)]
        wg_t = jnp.concatenate(rows, 0).astype(bf)
        gate_b = jnp.broadcast_to(jnp.concatenate(brow).reshape(3, 2, 8, 1), (3, 2, 8, LC))
        conv = jnp.concatenate([qk_conv[i], jnp.zeros((8 - CONV_WIDTH, 2 * WC), f32)], 0)
        sink = jnp.broadcast_to(sink_logits[i][jnp.array(_B_ORDER)].reshape(2, 2, 1, 1), (2, 2, 8, LANES))
        wo = w_out[i]
        wo = jnp.concatenate([wo[:WA], _perm_heads(wo[WA:WA + WBQ], _B_ORDER, 0), wo[WA + WBQ:]], axis=0).astype(bf)

        a1, a4, a16, qb, kb, vb, qc, kc, vc, oc, gt = _inproj(x, attn_norm[i][None], w_main, wg_t, conv)
        ya = _attn_a(a1.reshape(B, 1, S, 3 * WA), a4, a16, bias_a)
        yb = _attn_b(qb, kb, vb, bias_b, sink)
        hf, hb = _mlstm(qc, kc, vc, gt, gate_b, tri)
        x = _post(x, ya, yb, hf, hb, oc, p, i, mlstm_norm[i][None], head_mean, wo, ffn_norm[i][None], w_up[i].astype(bf), w_down[i].astype(bf),
                  ple_proj[i].astype(bf), ple_gate[i].astype(bf), final_norm[None], final=(i == depth - 1))
    return x
```

```python
import functools

import jax
import jax.numpy as jnp
from jax import lax
from jax.experimental import pallas as pl
from jax.experimental.pallas import tpu as pltpu

D_MODEL = 1024
HEAD_DIM = 64
HEADS_A = 6
HEADS_B = 4
KV_HEADS_B = 2
HEADS_C = 6
DILATIONS = (1, 4, 16)
HALF_A = 64
HALF_B = 128
CONV_WIDTH = 5
N_BUCKETS = 32
REL_MAX_DIST = 1024
D_FF = 2816
EPS = 1e-6
NEG = -1e30
WA = HEADS_A * HEAD_DIM
WBQ = HEADS_B * HEAD_DIM
WBK = KV_HEADS_B * HEAD_DIM
WC = HEADS_C * HEAD_DIM
D_MAIN = 3 * WA + WBQ + 2 * WBK + 4 * WC

LANES = 128
BF16_ROWS = 16
VMEM_LIMIT = 56 * 1024 * 1024

TM_IN = 512
RC_IN = 256
TM_POST = 512
FF_CHUNK = 256
TQ = 128
TB_A = 2048
TB_B = 1024
TILES_A = 8
TILES_B = 4
LC = 128
MLSTM_BATCH = 2
HALO = BF16_ROWS

_NT = (((1,), (1,)), ((), ()))
_TN = (((0,), (0,)), ((), ()))
_HI = lax.Precision.HIGHEST


def _rms(x, g):
    return x * lax.rsqrt(jnp.mean(x * x, axis=-1, keepdims=True) + EPS) * g


def _inproj_kernel(x_ref, xp_ref, xn_ref, g_ref, w_ref, wg_ref, wgc_ref, cw_ref,
                   a1_ref, a4_ref, a16_ref, qb_ref, kb_ref, vb_ref,
                   qc_ref, kc_ref, vc_ref, oc_ref, gt_ref, gc_ref, z_scr, c_scr):
    tm = x_ref.shape[0]
    i = pl.program_id(1)
    bf = jnp.bfloat16
    f32 = jnp.float32
    rc = RC_IN
    nl = 3 * WA // LANES
    b_lo = 3 * WA
    qk_lo = b_lo + WBQ + 2 * WBK
    vo_lo = qk_lo + 2 * WC

    hh = _rms(jnp.concatenate([xp_ref[...], xn_ref[...]], axis=0), g_ref[...]).astype(bf)
    zh = jnp.dot(hh, w_ref[:, qk_lo:vo_lo], preferred_element_type=f32)
    c_scr[0:HALO, :] = zh[0:HALO] * (i > 0).astype(f32)
    c_scr[HALO + tm:, :] = zh[HALO:] * (i < pl.num_programs(1) - 1).astype(f32)

    def conv_rows(r0):
        for s0 in range(r0, r0 + rc, 128):
            acc = jnp.zeros((128, 2 * WC), f32)
            for j in range(CONV_WIDTH):
                acc = acc + c_scr[pl.ds(s0 + HALO - CONV_WIDTH // 2 + j, 128), :] * cw_ref[j:j + 1, :]
            act = acc * jax.nn.sigmoid(acc)
            qc_ref[s0:s0 + 128, :] = act[:, :WC].astype(bf)
            kc_ref[s0:s0 + 128, :] = (act[:, WC:] * (HEAD_DIM ** -0.5)).astype(bf)

    for r0 in range(0, tm, rc):
        rows = slice(r0, r0 + rc)
        h = _rms(x_ref[rows, :], g_ref[...]).astype(bf)
        za = jnp.dot(h, w_ref[:, 0:b_lo], preferred_element_type=f32)
        a1_ref[rows, :] = za.astype(bf)
        for j in range(nl):
            z_scr[j, rows, :] = za[:, j * LANES:(j + 1) * LANES]
        for d, ref in ((4, a4_ref), (16, a16_ref)):
            for c in range(d):
                for j in range(nl):
                    ref[c, r0 // d:(r0 + rc) // d, j * LANES:(j + 1) * LANES] = \
                        z_scr[j, pl.ds(r0 + c, rc // d, stride=d), :].astype(bf)
        zb = jnp.dot(h, w_ref[:, b_lo:qk_lo], preferred_element_type=f32)
        qb_ref[rows, :] = zb[:, :WBQ].astype(bf)
        kb_ref[rows, :] = zb[:, WBQ:WBQ + WBK].astype(bf)
        vb_ref[rows, :] = zb[:, WBQ + WBK:].astype(bf)
        c_scr[HALO + r0:HALO + r0 + rc, :] = jnp.dot(h, w_ref[:, qk_lo:vo_lo], preferred_element_type=f32)
        zv = jnp.dot(h, w_ref[:, vo_lo:], preferred_element_type=f32)
        vc_ref[rows, :] = zv[:, :WC].astype(bf)
        oc_ref[rows, :] = zv[:, WC:].astype(bf)
        gt = lax.dot_general(wg_ref[...], h, _NT, preferred_element_type=f32)
        gt_ref[:, :, :, rows] = gt.reshape(gt_ref.shape[:-1] + (rc,))
        gc_ref[rows, :] = jnp.dot(h, wgc_ref[...], preferred_element_type=f32)
        if r0 > 0:
            conv_rows(r0 - rc)
    conv_rows(tm - rc)


def _inproj(x, g, w_main, wg_t, wg_c, conv_w):
    B, S, D = x.shape
    tm = TM_IN
    nt = S // tm
    hpb = tm // HALO
    nh = S // HALO
    bf = jnp.bfloat16
    f32 = jnp.float32

    def tok(width):
        return pl.BlockSpec((None, tm, width), lambda b, i: (b, i, 0))

    def cls(d):
        return pl.BlockSpec((None, d, tm // d, 3 * WA), lambda b, i: (b, 0, i, 0))

    out_shape = (
        jax.ShapeDtypeStruct((B, S, 3 * WA), bf),
        jax.ShapeDtypeStruct((B, 4, S // 4, 3 * WA), bf),
        jax.ShapeDtypeStruct((B, 16, S // 16, 3 * WA), bf),
        jax.ShapeDtypeStruct((B, S, WBQ), bf),
        jax.ShapeDtypeStruct((B, S, WBK), bf),
        jax.ShapeDtypeStruct((B, S, WBK), bf),
        jax.ShapeDtypeStruct((B, S, WC), bf),
        jax.ShapeDtypeStruct((B, S, WC), bf),
        jax.ShapeDtypeStruct((B, S, WC), bf),
        jax.ShapeDtypeStruct((B, S, WC), bf),
        jax.ShapeDtypeStruct((B, 3, 2, 8, S), f32),
        jax.ShapeDtypeStruct((B, S, LANES), f32),
    )
    out_specs = (
        tok(3 * WA), cls(4), cls(16), tok(WBQ), tok(WBK), tok(WBK),
        tok(WC), tok(WC), tok(WC), tok(WC),
        pl.BlockSpec((None, 3, 2, 8, tm), lambda b, i: (b, 0, 0, 0, i)),
        tok(LANES),
    )
    return pl.pallas_call(
        _inproj_kernel,
        grid=(B, nt),
        in_specs=[
            pl.BlockSpec((None, tm, D), lambda b, i: (b, i, 0)),
            pl.BlockSpec((None, HALO, D), lambda b, i: (b, jnp.maximum(i * hpb - 1, 0), 0)),
            pl.BlockSpec((None, HALO, D), lambda b, i: (b, jnp.minimum((i + 1) * hpb, nh - 1), 0)),
            pl.BlockSpec((1, D), lambda b, i: (0, 0)),
            pl.BlockSpec((D, D_MAIN), lambda b, i: (0, 0)),
            pl.BlockSpec((48, D), lambda b, i: (0, 0)),
            pl.BlockSpec((D, LANES), lambda b, i: (0, 0)),
            pl.BlockSpec(conv_w.shape, lambda b, i: (0, 0)),
        ],
        out_specs=out_specs,
        out_shape=out_shape,
        scratch_shapes=[pltpu.VMEM((3 * WA // LANES, tm, LANES), f32),
                        pltpu.VMEM((tm + 2 * HALO, 2 * WC), f32)],
        compiler_params=pltpu.CompilerParams(
            dimension_semantics=("parallel", "parallel"), vmem_limit_bytes=VMEM_LIMIT),
        name="inproj",
    )(x, x, x, g, w_main, wg_t, wg_c, conv_w)


def _band_tile(q2, kw, vw, bias0, bias1, lo, out, sinks=None):
    q2 = q2 * jnp.asarray(HEAD_DIM ** -0.5, q2.dtype)
    vext = jnp.concatenate([vw, jnp.ones_like(vw)], axis=1)
    zero = jnp.zeros_like(q2)
    logits = [lax.dot_general(jnp.where(lo if hh == 0 else jnp.logical_not(lo), q2, zero), kw, _NT,
                              preferred_element_type=jnp.float32) for hh in (0, 1)]
    yield
    for hh, bias in ((0, bias0), (1, bias1)):
        s = logits[hh] + bias
        m = jnp.max(s, axis=1, keepdims=True)
        if sinks is not None:
            m = jnp.maximum(m, sinks[hh])
        r = jnp.dot(jnp.exp(s - m).astype(jnp.bfloat16), vext, preferred_element_type=jnp.float32)
        yield
        out.append((m, r))


def _round_robin(gens):
    live = list(gens)
    while live:
        live = [g for g in live if next(g, StopIteration) is not StopIteration]


def _attn_a_kernel(q1_ref, q4_ref, q16_ref, k1_ref, v1_ref, k4_ref, v4_ref, k16_ref, v16_ref,
                   bias_ref, o_ref, acc_scr, m_scr, l_scr, *, seq):
    tb = o_ref.shape[0]
    tbi = pl.program_id(2)
    lo = lax.broadcasted_iota(jnp.int32, (TQ, LANES), 1) < HEAD_DIM
    W = TQ + 2 * HALF_A
    groups = ((1, q1_ref, k1_ref, v1_ref), (4, q4_ref, k4_ref, v4_ref), (16, q16_ref, k16_ref, v16_ref))
    for g, (d, q_ref, k_ref, v_ref) in enumerate(groups):
        L = seq // d
        nsub = tb // d
        ntile = nsub // TQ

        def tiles(it, carry, g=g, d=d, q_ref=q_ref, k_ref=k_ref, v_ref=v_ref, L=L, nsub=nsub, ntile=ntile):
            work = []
            for u in range(TILES_A):
                idx = it * TILES_A + u
                c = idx // ntile
                jj = idx % ntile
                sub0 = tbi * nsub + jj * TQ
                ws = pl.multiple_of(jnp.clip(sub0 - HALF_A, 0, L - W), HALF_A)
                var = (sub0 - ws) // HALF_A
                q2 = q_ref[c, pl.ds(pl.multiple_of(jj * TQ, TQ), TQ), :]
                kw = k_ref[c, pl.ds(ws, W), :]
                vw = v_ref[c, pl.ds(ws, W), :]
                out = []
                rows = pl.ds(jj * (TQ * d) + c, TQ, stride=d) if d > 1 else pl.ds(pl.multiple_of(jj * TQ, TQ), TQ)
                work.append((_band_tile(q2, kw, vw, bias_ref[g, var, 0], bias_ref[g, var, 1], lo, out), out, rows))
            _round_robin([w[0] for w in work])
            for _, ((m0, r0), (m1, r1)), rows in work:
                acc_scr[g, rows, :] = jnp.where(lo, r0[:, :LANES], r1[:, :LANES])
                l_scr[g, rows, :] = jnp.where(lo, r0[:, LANES:], r1[:, LANES:])
                m_scr[g, rows, :] = jnp.where(lo, m0, m1)
            return carry

        lax.fori_loop(0, d * ntile // TILES_A, tiles, 0)

    rc = 256

    def combine(i, carry):
        rows = pl.ds(pl.multiple_of(i * rc, rc), rc)
        m0, m1, m2 = m_scr[0, rows, :], m_scr[1, rows, :], m_scr[2, rows, :]
        mm = jnp.maximum(jnp.maximum(m0, m1), m2)
        e0, e1, e2 = jnp.exp(m0 - mm), jnp.exp(m1 - mm), jnp.exp(m2 - mm)
        num = e0 * acc_scr[0, rows, :] + e1 * acc_scr[1, rows, :] + e2 * acc_scr[2, rows, :]
        den = e0 * l_scr[0, rows, :] + e1 * l_scr[1, rows, :] + e2 * l_scr[2, rows, :]
        o_ref[rows, :] = (num / den).astype(o_ref.dtype)
        return carry

    lax.fori_loop(0, tb // rc, combine, 0)


def _attn_a(a1, a4, a16, bias):
    B, _, S, _ = a1.shape
    tb = TB_A
    npair = HEADS_A // 2

    def qspec(d):
        return pl.BlockSpec((None, d, tb // d, LANES), lambda b, p, t: (b, 0, t, p))

    def kvspec(d, off):
        return pl.BlockSpec((None, d, S // d, LANES), lambda b, p, t, off=off: (b, 0, 0, off + p))

    in_specs = [qspec(1), qspec(4), qspec(16)]
    args = [a1, a4, a16]
    for d, arr in ((1, a1), (4, a4), (16, a16)):
        in_specs += [kvspec(d, npair), kvspec(d, 2 * npair)]
        args += [arr, arr]
    in_specs.append(pl.BlockSpec((3, 3, 2, TQ, TQ + 2 * HALF_A), lambda b, p, t: (0, 0, p, 0, 0)))
    args.append(bias)
    return pl.pallas_call(
        functools.partial(_attn_a_kernel, seq=S),
        grid=(B, npair, S // tb),
        in_specs=in_specs,
        out_specs=pl.BlockSpec((None, tb, LANES), lambda b, p, t: (b, t, p)),
        out_shape=jax.ShapeDtypeStruct((B, S, WA), jnp.bfloat16),
        scratch_shapes=[pltpu.VMEM((3, tb, LANES), jnp.float32)] * 3,
        compiler_params=pltpu.CompilerParams(
            dimension_semantics=("parallel", "parallel", "arbitrary"), vmem_limit_bytes=VMEM_LIMIT),
        name="attn_a",
    )(*args)


def _attn_b_kernel(q_ref, k_ref, v_ref, bias_ref, sink_ref, o_ref, *, seq):
    tb = o_ref.shape[0]
    tbi = pl.program_id(2)
    lo = lax.broadcasted_iota(jnp.int32, (TQ, LANES), 1) < HEAD_DIM
    W = TQ + 2 * HALF_B
    s0 = sink_ref[0, 0:1, 0:1]
    s1 = sink_ref[1, 0:1, 0:1]

    def tiles(it, carry):
        work = []
        for u in range(TILES_B):
            jj = it * TILES_B + u
            sub0 = tbi * tb + jj * TQ
            ws = pl.multiple_of(jnp.clip(sub0 - HALF_B, 0, seq - W), HALF_B)
            var = (sub0 - ws) // HALF_B
            rows = pl.ds(pl.multiple_of(jj * TQ, TQ), TQ)
            out = []
            work.append((_band_tile(q_ref[rows, :], k_ref[pl.ds(ws, W), :], v_ref[pl.ds(ws, W), :],
                                    bias_ref[var, 0], bias_ref[var, 1], lo, out, sinks=(s0, s1)), out, rows))
        _round_robin([w[0] for w in work])
        for _, ((m0, r0), (m1, r1)), rows in work:
            num = jnp.where(lo, r0[:, :LANES], r1[:, :LANES])
            den = jnp.where(lo, r0[:, LANES:] + jnp.exp(s0 - m0), r1[:, LANES:] + jnp.exp(s1 - m1))
            o_ref[rows, :] = (num / den).astype(o_ref.dtype)
        return carry

    lax.fori_loop(0, tb // TQ // TILES_B, tiles, 0)


def _attn_b(qb, kb, vb, bias, sink):
    B, S, _ = qb.shape
    tb = TB_B
    npair = HEADS_B // 2
    return pl.pallas_call(
        functools.partial(_attn_b_kernel, seq=S),
        grid=(B, npair, S // tb),
        in_specs=[
            pl.BlockSpec((None, tb, LANES), lambda b, p, t: (b, t, p)),
            pl.BlockSpec((None, S, LANES), lambda b, p, t: (b, 0, 0)),
            pl.BlockSpec((None, S, LANES), lambda b, p, t: (b, 0, 0)),
            pl.BlockSpec((3, None, 2, TQ, TQ + 2 * HALF_B), lambda b, p, t: (0, p, 0, 0, 0)),
            pl.BlockSpec((None, 2, 8, LANES), lambda b, p, t: (p, 0, 0, 0)),
        ],
        out_specs=pl.BlockSpec((None, tb, LANES), lambda b, p, t: (b, t, p)),
        out_shape=jax.ShapeDtypeStruct((B, S, WBQ), jnp.bfloat16),
        compiler_params=pltpu.CompilerParams(
            dimension_semantics=("parallel", "parallel", "arbitrary"), vmem_limit_bytes=VMEM_LIMIT),
        name="attn_b",
    )(qb, kb, vb, bias, sink)


def _split3(x):
    bf = jnp.bfloat16
    p1 = x.astype(bf).astype(jnp.float32)
    r1 = x - p1
    p2 = r1.astype(bf).astype(jnp.float32)
    return p1, p2, (r1 - p2).astype(bf).astype(jnp.float32)


def _mlstm_chain(q2, k2, vext, gt, lf, bcs, ics, tri, state, mrow, lo, out):
    f32 = jnp.float32
    bf = jnp.bfloat16
    b_row = sum(lax.dot_general(p.astype(bf), tri, _NT, preferred_element_type=f32) for p in _split3(lf))
    zero = jnp.zeros_like(q2)
    qks = [lax.dot_general(jnp.where(lo if hh == 0 else jnp.logical_not(lo), q2, zero), k2, _NT,
                           preferred_element_type=f32) for hh in (0, 1)]
    inter = jnp.dot(q2, state.astype(bf), preferred_element_type=f32)
    yield
    b_tot = jnp.sum(lf, axis=1, keepdims=True)
    scal = []
    for hh in (0, 1):
        br = b_row[2 + hh:3 + hh, :]
        ir = gt[hh:hh + 1, :]
        bt = b_tot[2 + hh:3 + hh, :]
        mp = mrow[hh:hh + 1, :]
        m_loc = jnp.max(bt - br + ir, axis=1, keepdims=True)
        m_new = jnp.maximum(bt + mp, m_loc)
        scal.append((br, ir, mp, bt - m_new, jnp.exp(bt + mp - m_new), m_new))
    visible = tri > 0
    heads = []
    for hh in (0, 1):
        br, ir, mp = scal[hh][:3]
        bc = bcs[hh]
        dmat = jnp.where(visible, bc - br + ir, NEG)
        m_in = bc + mp
        m_t = jnp.maximum(jnp.max(dmat, axis=1, keepdims=True), m_in)
        p = (jnp.exp(dmat - m_t) * qks[hh]).astype(bf)
        r = jnp.dot(p, vext, preferred_element_type=f32)
        yield
        heads.append((r, jnp.exp(m_in - m_t), m_t))
    (r0, a0, mt0), (r1, a1, mt1) = heads
    e2 = jnp.exp(jnp.where(lo, ics[0] - bcs[0] + scal[0][3], ics[1] - bcs[1] + scal[1][3]))
    ke = (k2.astype(f32) * e2).astype(bf)
    s_loc = lax.dot_general(ke, vext, _TN, preferred_element_type=f32)
    yield

    a2 = jnp.where(lo, a0, a1)
    mt2 = jnp.where(lo, mt0, mt1)
    num = jnp.where(lo, r0[:, :LANES], r1[:, :LANES]) + a2 * inter[:, :LANES]
    den = jnp.where(lo, r0[:, LANES:], r1[:, LANES:]) + a2 * inter[:, LANES:]
    h = num / jnp.maximum(jnp.abs(den), jnp.exp(-mt2))

    row = lax.broadcasted_iota(jnp.int32, (LANES, 2 * LANES), 0)
    col = lax.broadcasted_iota(jnp.int32, (LANES, 2 * LANES), 1)
    same_head = (row < HEAD_DIM) == ((col % LANES) < HEAD_DIM)
    dec_rows = jnp.where(row < HEAD_DIM, scal[0][4][:, 0:1], scal[1][4][:, 0:1])
    new_state = dec_rows * state + jnp.where(same_head, s_loc, 0.0)
    out.append((h, new_state, scal[0][5], scal[1][5]))


def _mlstm_kernel(fq, fk, fv, bq, bk, bv, gtf_ref, gtb_ref, gcf_ref, gcb_ref, gbias_ref, gbcol_ref, tri_ref,
                  hf_ref, hb_ref, st_scr, m_scr):
    c = pl.program_id(1)
    f32 = jnp.float32
    bf = jnp.bfloat16
    lo = lax.broadcasted_iota(jnp.int32, (LC, LANES), 1) < HEAD_DIM

    @pl.when(c == 0)
    def _():
        st_scr[...] = jnp.zeros_like(st_scr)
        m_scr[...] = jnp.zeros_like(m_scr)

    dirs = ((fq, fk, fv, gtf_ref, gcf_ref, hf_ref), (bq, bk, bv, gtb_ref, gcb_ref, hb_ref))
    chains = []
    for bi in range(fq.shape[0]):
        for drn, (q_ref, k_ref, v_ref, gt_ref, gc_ref, h_ref) in enumerate(dirs):
            tri = tri_ref[drn]
            gt_all = (gt_ref[bi] + gbias_ref[:, drn]).reshape(3 * 8, LC)
            lf_all = jnp.minimum(gt_all, 0.0) - jnp.log(1.0 + jnp.exp(-jnp.abs(gt_all)))
            bcol = sum(lax.dot_general(tri, p.astype(bf), _NT, preferred_element_type=f32) for p in _split3(lf_all))
            gcol = gc_ref[bi] + gbcol_ref[...]
            for pr in range(HEADS_C // 2):
                sl = slice(pr * LANES, (pr + 1) * LANES)
                v2 = v_ref[bi, :, sl]
                vext = jnp.concatenate([v2, jnp.ones_like(v2)], axis=1)
                base = (pr * 2 + drn) * 8
                ics = [jnp.broadcast_to(gcol[:, base + hh:base + hh + 1], (LC, LANES)) for hh in (0, 1)]
                bcs = [jnp.broadcast_to(bcol[:, pr * 8 + 2 + hh:pr * 8 + 3 + hh], (LC, LANES)) for hh in (0, 1)]
                out = []
                gen = _mlstm_chain(q_ref[bi, :, sl], k_ref[bi, :, sl], vext, gt_all[pr * 8:(pr + 1) * 8],
                                   lf_all[pr * 8:(pr + 1) * 8], bcs, ics, tri,
                                   st_scr[bi, drn, pr], m_scr[bi, drn, pr], lo, out)
                chains.append((gen, out, (bi, drn, pr), sl, h_ref))

    _round_robin([ch[0] for ch in chains])
    for _, out, (bi, drn, pr), sl, h_ref in chains:
        h, new_state, mn0, mn1 = out[0]
        st_scr[bi, drn, pr] = new_state
        m_scr[bi, drn, pr, 0:1, :] = mn0
        m_scr[bi, drn, pr, 1:2, :] = mn1
        h_ref[bi, :, sl] = h.astype(h_ref.dtype)


def _mlstm(qc, kc, vc, gt, gcol, gate_b, gate_bcol, tri):
    B, S, _ = qc.shape
    nc = S // LC
    npair = HEADS_C // 2

    def pos(drn, c):
        return c if drn == 0 else nc - 1 - c

    nb = MLSTM_BATCH

    def cur(drn):
        return pl.BlockSpec((nb, LC, WC), lambda b, c: (b, pos(drn, c), 0))

    def gates(drn):
        return pl.BlockSpec((nb, npair, None, 8, LC), lambda b, c: (b, 0, drn, 0, pos(drn, c)))

    def const(shape):
        return pl.BlockSpec(shape, lambda b, c: (0,) * len(shape))

    in_specs, args = [], []
    for drn in (0, 1):
        in_specs += [cur(drn), cur(drn), cur(drn)]
        args += [qc, kc, vc]
    in_specs += [gates(0), gates(1), pl.BlockSpec((nb, LC, LANES), lambda b, c: (b, pos(0, c), 0)),
                 pl.BlockSpec((nb, LC, LANES), lambda b, c: (b, pos(1, c), 0)),
                 const(gate_b.shape), const(gate_bcol.shape), const(tri.shape)]
    args += [gt, gt, gcol, gcol, gate_b, gate_bcol, tri]
    h_shape = jax.ShapeDtypeStruct((B, S, WC), jnp.bfloat16)
    return pl.pallas_call(
        _mlstm_kernel,
        grid=(B // nb, nc),
        in_specs=in_specs,
        out_specs=(cur(0), cur(1)),
        out_shape=(h_shape, h_shape),
        scratch_shapes=[
            pltpu.VMEM((nb, 2, npair, LANES, 2 * LANES), jnp.float32),
            pltpu.VMEM((nb, 2, npair, 8, LANES), jnp.float32),
        ],
        compiler_params=pltpu.CompilerParams(
            dimension_semantics=("parallel", "arbitrary"), vmem_limit_bytes=VMEM_LIMIT),
        name="mlstm",
    )(*args)


def _post_kernel(x_ref, ya_ref, yb_ref, hf_ref, hb_ref, oc_ref, p_ref, ng_ref, hm_ref, wo_ref, fg_ref,
                 wup_ref, wdn_ref, pp_ref, pg_ref, fin_ref, out_ref, *, final):
    f32 = jnp.float32
    bf = jnp.bfloat16
    hs = hf_ref[...].astype(f32) + hb_ref[...].astype(f32)
    sq = hs * hs
    sq_hi = sq.astype(bf)
    sq_lo = (sq - sq_hi.astype(f32)).astype(bf)
    ms = (jnp.dot(sq_hi, hm_ref[...], preferred_element_type=f32)
          + jnp.dot(sq_lo, hm_ref[...], preferred_element_type=f32))
    yc = (jax.nn.sigmoid(oc_ref[...].astype(f32)) * (hs * lax.rsqrt(ms + EPS) * ng_ref[...])).astype(bf)
    y = jnp.concatenate([ya_ref[...], yb_ref[...], yc], axis=1)
    x1 = x_ref[...] + jnp.dot(y, wo_ref[...], preferred_element_type=f32)
    h = _rms(x1, fg_ref[...]).astype(bf)
    ffn = None
    for j in range(D_FF // FF_CHUNK):
        gate = jnp.dot(h, wup_ref[:, j * FF_CHUNK:(j + 1) * FF_CHUNK], preferred_element_type=f32)
        up = jnp.dot(h, wup_ref[:, D_FF + j * FF_CHUNK:D_FF + (j + 1) * FF_CHUNK], preferred_element_type=f32)
        act = (gate * jax.nn.sigmoid(gate) * up).astype(bf)
        part = jnp.dot(act, wdn_ref[j * FF_CHUNK:(j + 1) * FF_CHUNK, :], preferred_element_type=f32)
        ffn = part if ffn is None else ffn + part
    x2 = x1 + ffn
    emb = jnp.dot(p_ref[...].astype(bf), pp_ref[...], preferred_element_type=f32)
    gate = jax.nn.sigmoid(jnp.dot(x2.astype(bf), pg_ref[...], preferred_element_type=f32))
    x3 = x2 + emb * gate
    if final:
        x3 = _rms(x3, fin_ref[...])
    out_ref[...] = x3


def _post(x, ya, yb, hf, hb, oc, p, layer, ng, hm, wo, fg, wup, wdn, pp, pg, fin, final):
    B, S, D = x.shape
    tm = TM_POST

    def tok(width):
        return pl.BlockSpec((None, tm, width), lambda b, i: (b, i, 0))

    def const(shape):
        return pl.BlockSpec(shape, lambda b, i: (0,) * len(shape), pipeline_mode=pl.Buffered(1))

    return pl.pallas_call(
        functools.partial(_post_kernel, final=final),
        grid=(B, S // tm),
        in_specs=[
            tok(D), tok(WA), tok(WBQ), tok(WC), tok(WC), tok(WC),
            pl.BlockSpec((None, None, tm, p.shape[-1]), lambda b, i: (layer, b, i, 0)),
            const(ng.shape), const(hm.shape), const(wo.shape), const((1, D)),
            const(wup.shape), const(wdn.shape), const(pp.shape), const(pg.shape), const((1, D)),
        ],
        out_specs=tok(D),
        out_shape=jax.ShapeDtypeStruct((B, S, D), jnp.float32),
        compiler_params=pltpu.CompilerParams(
            dimension_semantics=("parallel", "parallel"), vmem_limit_bytes=VMEM_LIMIT),
        name="post",
    )(x, ya, yb, hf, hb, oc, p, ng, hm, wo, fg, wup, wdn, pp, pg, fin)


def _t5_bucket(rel):
    half_b = N_BUCKETS // 2
    max_exact = half_b // 2
    n = jnp.abs(rel)
    nf = jnp.maximum(n, 1).astype(jnp.float32)
    large = max_exact + (jnp.log(nf / max_exact) / jnp.log(jnp.float32(REL_MAX_DIST / max_exact))
                         * (half_b - max_exact)).astype(jnp.int32)
    large = jnp.minimum(large, half_b - 1)
    return jnp.where(rel > 0, half_b, 0) + jnp.where(n < max_exact, n, large)


def _band_bias(table, half, dil):
    W = TQ + 2 * half
    H = table.shape[1]
    L = W + TQ
    j = jnp.arange(L, dtype=jnp.int32)
    off = jnp.where(j < W, j, j - L)
    tiles = []
    for shift in (0, half, 2 * half):
        rel = off - shift
        vals = jnp.where((jnp.abs(rel) <= half)[:, None], table[_t5_bucket(rel * dil)].astype(jnp.float32), NEG)
        row = jnp.broadcast_to(vals.T[:, None, :], (H, TQ, L)).reshape(H, TQ * L)
        tiles.append(row[:, :TQ * (L - 1)].reshape(H, TQ, L - 1)[:, :, :W])
    return jnp.stack(tiles, 0)


_B_ORDER = (0, 2, 1, 3)


def _perm_heads(w, order, axis):
    blocks = [lax.slice_in_dim(w, h * HEAD_DIM, (h + 1) * HEAD_DIM, axis=axis) for h in order]
    return jnp.concatenate(blocks, axis=axis)


def kernel(x, p, rel_bias, attn_norm, w_in, qk_conv, gate_bias, sink_logits, mlstm_norm, w_out, ffn_norm,
           w_up, w_down, ple_proj, ple_gate, final_norm):
    B, S, D = x.shape
    depth = w_in.shape[0]
    assert D == D_MODEL and S % TB_A == 0 and S // 16 >= TQ + 2 * HALF_A and B % MLSTM_BATCH == 0
    f32 = jnp.float32
    bf = jnp.bfloat16

    bias_a = jnp.stack([_band_bias(rel_bias[:, :HEADS_A], HALF_A, d) for d in DILATIONS], 0)
    bias_b = _band_bias(rel_bias[:, HEADS_A:], HALF_B, 1)
    bias_b = jnp.stack([bias_b[:, h] for h in _B_ORDER], 1).reshape(3, 2, 2, TQ, TQ + 2 * HALF_B)
    tri_lo = (jnp.arange(LC)[:, None] >= jnp.arange(LC)[None, :]).astype(f32)
    tri = jnp.stack([tri_lo, tri_lo.T], 0).astype(bf)
    head_id = jnp.arange(WC) // HEAD_DIM
    head_mean = ((head_id[:, None] == head_id[None, :]).astype(f32) / HEAD_DIM).astype(bf)

    gate_off = D_MAIN
    for i in range(depth):
        w = w_in[i]
        qb_off = 3 * WA
        w_main = jnp.concatenate([
            w[:, :qb_off], _perm_heads(w[:, qb_off:qb_off + WBQ], _B_ORDER, 1), w[:, qb_off + WBQ:gate_off]],
            axis=1).astype(bf)
        wg = w[:, gate_off:]
        gb = gate_bias[i]
        rows, brow = [], []
        zrow = jnp.zeros((4, D), f32)
        for pr in range(HEADS_C // 2):
            for drn in range(2):
                idx = [(2 * drn) * HEADS_C + 2 * pr, (2 * drn) * HEADS_C + 2 * pr + 1,
                       (2 * drn + 1) * HEADS_C + 2 * pr, (2 * drn + 1) * HEADS_C + 2 * pr + 1]
                rows += [wg[:, idx].T, zrow]
                brow += [gb[jnp.array(idx)], jnp.zeros((4,), f32)]
        wg_t = jnp.concatenate(rows, 0).astype(bf)
        wg_c = jnp.concatenate([wg_t.T, jnp.zeros((D, LANES - 48), bf)], axis=1)
        gate_b = jnp.broadcast_to(jnp.concatenate(brow).reshape(3, 2, 8, 1), (3, 2, 8, LC))
        gate_bcol = jnp.concatenate([jnp.concatenate(brow), jnp.zeros((LANES - 48,), f32)])[None]
        conv = jnp.concatenate([qk_conv[i], jnp.zeros((8 - CONV_WIDTH, 2 * WC), f32)], 0)
        sink = jnp.broadcast_to(sink_logits[i][jnp.array(_B_ORDER)].reshape(2, 2, 1, 1), (2, 2, 8, LANES))
        wo = w_out[i]
        wo = jnp.concatenate([wo[:WA], _perm_heads(wo[WA:WA + WBQ], _B_ORDER, 0), wo[WA + WBQ:]], axis=0).astype(bf)

        a1, a4, a16, qb, kb, vb, qc, kc, vc, oc, gt, gcol = _inproj(x, attn_norm[i][None], w_main, wg_t, wg_c, conv)
        ya = _attn_a(a1.reshape(B, 1, S, 3 * WA), a4, a16, bias_a)
        yb = _attn_b(qb, kb, vb, bias_b, sink)
        hf, hb = _mlstm(qc, kc, vc, gt, gcol, gate_b, gate_bcol, tri)
        x = _post(x, ya, yb, hf, hb, oc, p, i, mlstm_norm[i][None], head_mean, wo, ffn_norm[i][None], w_up[i].astype(bf), w_down[i].astype(bf),
                  ple_proj[i].astype(bf), ple_gate[i].astype(bf), final_norm[None], final=(i == depth - 1))
    return x
```

```python
import functools

import jax
import jax.numpy as jnp
from jax import lax
from jax.experimental import pallas as pl
from jax.experimental.pallas import tpu as pltpu

D_MODEL = 1024
HEAD_DIM = 64
HEADS_A = 6
HEADS_B = 4
KV_HEADS_B = 2
HEADS_C = 6
DILATIONS = (1, 4, 16)
HALF_A = 64
HALF_B = 128
CONV_WIDTH = 5
N_BUCKETS = 32
REL_MAX_DIST = 1024
D_FF = 2816
EPS = 1e-6
NEG = -1e30
WA = HEADS_A * HEAD_DIM
WBQ = HEADS_B * HEAD_DIM
WBK = KV_HEADS_B * HEAD_DIM
WC = HEADS_C * HEAD_DIM
D_MAIN = 3 * WA + WBQ + 2 * WBK + 4 * WC

LANES = 128
BF16_ROWS = 16
VMEM_LIMIT = 56 * 1024 * 1024

TM_IN = 512
RC_IN = 256
TM_POST = 512
FF_CHUNK = 256
TQ = 128
TB_A = 2048
TB_B = 1024
TILES_A = 8
TILES_B = 4
LC = 128
MLSTM_BATCH = 2
HALO = BF16_ROWS

_NT = (((1,), (1,)), ((), ()))
_TN = (((0,), (0,)), ((), ()))
_HI = lax.Precision.HIGHEST


def _rms(x, g):
    return x * lax.rsqrt(jnp.mean(x * x, axis=-1, keepdims=True) + EPS) * g


def _sigmoid(x):
    return 0.5 * jnp.tanh(0.5 * x) + 0.5


def _inproj_kernel(x_ref, xp_ref, xn_ref, g_ref, w_ref, wg_ref, wgc_ref, cw_ref,
                   a1_ref, a4_ref, a16_ref, qb_ref, kb_ref, vb_ref,
                   qc_ref, kc_ref, vc_ref, oc_ref, gt_ref, gc_ref, z_scr, c_scr):
    tm = x_ref.shape[0]
    i = pl.program_id(1)
    bf = jnp.bfloat16
    f32 = jnp.float32
    rc = RC_IN
    nl = 3 * WA // LANES
    b_lo = 3 * WA
    qk_lo = b_lo + WBQ + 2 * WBK
    vo_lo = qk_lo + 2 * WC

    hh = _rms(jnp.concatenate([xp_ref[...], xn_ref[...]], axis=0), g_ref[...]).astype(bf)
    zh = jnp.dot(hh, w_ref[:, qk_lo:vo_lo], preferred_element_type=f32)
    c_scr[0:HALO, :] = zh[0:HALO] * (i > 0).astype(f32)
    c_scr[HALO + tm:, :] = zh[HALO:] * (i < pl.num_programs(1) - 1).astype(f32)

    def conv_rows(r0):
        for s0 in range(r0, r0 + rc, 128):
            acc = jnp.zeros((128, 2 * WC), f32)
            for j in range(CONV_WIDTH):
                acc = acc + c_scr[pl.ds(s0 + HALO - CONV_WIDTH // 2 + j, 128), :] * cw_ref[j:j + 1, :]
            act = acc * _sigmoid(acc)
            qc_ref[s0:s0 + 128, :] = act[:, :WC].astype(bf)
            kc_ref[s0:s0 + 128, :] = (act[:, WC:] * (HEAD_DIM ** -0.5)).astype(bf)

    for r0 in range(0, tm, rc):
        rows = slice(r0, r0 + rc)
        h = _rms(x_ref[rows, :], g_ref[...]).astype(bf)
        za = jnp.dot(h, w_ref[:, 0:b_lo], preferred_element_type=f32)
        a1_ref[rows, :] = za.astype(bf)
        for j in range(nl):
            z_scr[j, rows, :] = za[:, j * LANES:(j + 1) * LANES]
        for d, ref in ((4, a4_ref), (16, a16_ref)):
            for c in range(d):
                for j in range(nl):
                    ref[c, r0 // d:(r0 + rc) // d, j * LANES:(j + 1) * LANES] = \
                        z_scr[j, pl.ds(r0 + c, rc // d, stride=d), :].astype(bf)
        zb = jnp.dot(h, w_ref[:, b_lo:qk_lo], preferred_element_type=f32)
        qb_ref[rows, :] = zb[:, :WBQ].astype(bf)
        kb_ref[rows, :] = zb[:, WBQ:WBQ + WBK].astype(bf)
        vb_ref[rows, :] = zb[:, WBQ + WBK:].astype(bf)
        c_scr[HALO + r0:HALO + r0 + rc, :] = jnp.dot(h, w_ref[:, qk_lo:vo_lo], preferred_element_type=f32)
        zv = jnp.dot(h, w_ref[:, vo_lo:], preferred_element_type=f32)
        vc_ref[rows, :] = zv[:, :WC].astype(bf)
        oc_ref[rows, :] = zv[:, WC:].astype(bf)
        gt = lax.dot_general(wg_ref[...], h, _NT, preferred_element_type=f32)
        gt_ref[:, :, :, rows] = gt.reshape(gt_ref.shape[:-1] + (rc,))
        gc_ref[rows, :] = jnp.dot(h, wgc_ref[...], preferred_element_type=f32)
        if r0 > 0:
            conv_rows(r0 - rc)
    conv_rows(tm - rc)


def _inproj(x, g, w_main, wg_t, wg_c, conv_w):
    B, S, D = x.shape
    tm = TM_IN
    nt = S // tm
    hpb = tm // HALO
    nh = S // HALO
    bf = jnp.bfloat16
    f32 = jnp.float32

    def tok(width):
        return pl.BlockSpec((None, tm, width), lambda b, i: (b, i, 0))

    def cls(d):
        return pl.BlockSpec((None, d, tm // d, 3 * WA), lambda b, i: (b, 0, i, 0))

    out_shape = (
        jax.ShapeDtypeStruct((B, S, 3 * WA), bf),
        jax.ShapeDtypeStruct((B, 4, S // 4, 3 * WA), bf),
        jax.ShapeDtypeStruct((B, 16, S // 16, 3 * WA), bf),
        jax.ShapeDtypeStruct((B, S, WBQ), bf),
        jax.ShapeDtypeStruct((B, S, WBK), bf),
        jax.ShapeDtypeStruct((B, S, WBK), bf),
        jax.ShapeDtypeStruct((B, S, WC), bf),
        jax.ShapeDtypeStruct((B, S, WC), bf),
        jax.ShapeDtypeStruct((B, S, WC), bf),
        jax.ShapeDtypeStruct((B, S, WC), bf),
        jax.ShapeDtypeStruct((B, 3, 2, 8, S), f32),
        jax.ShapeDtypeStruct((B, S, LANES), f32),
    )
    out_specs = (
        tok(3 * WA), cls(4), cls(16), tok(WBQ), tok(WBK), tok(WBK),
        tok(WC), tok(WC), tok(WC), tok(WC),
        pl.BlockSpec((None, 3, 2, 8, tm), lambda b, i: (b, 0, 0, 0, i)),
        tok(LANES),
    )
    return pl.pallas_call(
        _inproj_kernel,
        grid=(B, nt),
        in_specs=[
            pl.BlockSpec((None, tm, D), lambda b, i: (b, i, 0)),
            pl.BlockSpec((None, HALO, D), lambda b, i: (b, jnp.maximum(i * hpb - 1, 0), 0)),
            pl.BlockSpec((None, HALO, D), lambda b, i: (b, jnp.minimum((i + 1) * hpb, nh - 1), 0)),
            pl.BlockSpec((1, D), lambda b, i: (0, 0)),
            pl.BlockSpec((D, D_MAIN), lambda b, i: (0, 0)),
            pl.BlockSpec((48, D), lambda b, i: (0, 0)),
            pl.BlockSpec((D, LANES), lambda b, i: (0, 0)),
            pl.BlockSpec(conv_w.shape, lambda b, i: (0, 0)),
        ],
        out_specs=out_specs,
        out_shape=out_shape,
        scratch_shapes=[pltpu.VMEM((3 * WA // LANES, tm, LANES), f32),
                        pltpu.VMEM((tm + 2 * HALO, 2 * WC), f32)],
        compiler_params=pltpu.CompilerParams(
            dimension_semantics=("parallel", "parallel"), vmem_limit_bytes=VMEM_LIMIT),
        name="inproj",
    )(x, x, x, g, w_main, wg_t, wg_c, conv_w)


def _band_tile(q2, kw, vw, bias, lo, out, sinks=None):
    tq = q2.shape[0]
    q2 = q2 * jnp.asarray(HEAD_DIM ** -0.5, q2.dtype)
    vext = jnp.concatenate([vw, jnp.ones_like(vw)], axis=1)
    zero = jnp.zeros_like(q2)
    qs = jnp.concatenate([jnp.where(lo, q2, zero), jnp.where(lo, zero, q2)], axis=0)
    s = lax.dot_general(qs, kw, _NT, preferred_element_type=jnp.float32)
    yield
    s = s + bias.reshape(2 * tq, bias.shape[-1])
    m = jnp.max(s, axis=1, keepdims=True)
    if sinks is not None:
        m = jnp.maximum(m, jnp.concatenate([jnp.broadcast_to(sk, (tq, 1)) for sk in sinks], axis=0))
    r = jnp.dot(jnp.exp(s - m).astype(jnp.bfloat16), vext, preferred_element_type=jnp.float32)
    yield
    out.extend([(m[:tq], r[:tq]), (m[tq:], r[tq:])])


def _round_robin(gens):
    live = list(gens)
    while live:
        live = [g for g in live if next(g, StopIteration) is not StopIteration]


def _attn_a_kernel(q1_ref, q4_ref, q16_ref, k1_ref, v1_ref, k4_ref, v4_ref, k16_ref, v16_ref,
                   bias_ref, o_ref, acc_scr, m_scr, l_scr, *, seq):
    tb = o_ref.shape[0]
    tbi = pl.program_id(2)
    lo = lax.broadcasted_iota(jnp.int32, (TQ, LANES), 1) < HEAD_DIM
    W = TQ + 2 * HALF_A
    groups = ((1, q1_ref, k1_ref, v1_ref), (4, q4_ref, k4_ref, v4_ref), (16, q16_ref, k16_ref, v16_ref))
    for g, (d, q_ref, k_ref, v_ref) in enumerate(groups):
        L = seq // d
        nsub = tb // d
        ntile = nsub // TQ

        def tiles(it, carry, g=g, d=d, q_ref=q_ref, k_ref=k_ref, v_ref=v_ref, L=L, nsub=nsub, ntile=ntile):
            work = []
            for u in range(TILES_A):
                idx = it * TILES_A + u
                c = idx // ntile
                jj = idx % ntile
                sub0 = tbi * nsub + jj * TQ
                ws = pl.multiple_of(jnp.clip(sub0 - HALF_A, 0, L - W), HALF_A)
                var = (sub0 - ws) // HALF_A
                q2 = q_ref[c, pl.ds(pl.multiple_of(jj * TQ, TQ), TQ), :]
                kw = k_ref[c, pl.ds(ws, W), :]
                vw = v_ref[c, pl.ds(ws, W), :]
                out = []
                rows = pl.ds(jj * (TQ * d) + c, TQ, stride=d) if d > 1 else pl.ds(pl.multiple_of(jj * TQ, TQ), TQ)
                work.append((_band_tile(q2, kw, vw, bias_ref[g, var], lo, out), out, rows))
            _round_robin([w[0] for w in work])
            for _, ((m0, r0), (m1, r1)), rows in work:
                acc_scr[g, rows, :] = jnp.where(lo, r0[:, :LANES], r1[:, :LANES])
                l_scr[g, rows, :] = jnp.where(lo, r0[:, LANES:], r1[:, LANES:])
                m_scr[g, rows, :] = jnp.where(lo, m0, m1)
            return carry

        lax.fori_loop(0, d * ntile // TILES_A, tiles, 0)

    rc = 256

    def combine(i, carry):
        rows = pl.ds(pl.multiple_of(i * rc, rc), rc)
        m0, m1, m2 = m_scr[0, rows, :], m_scr[1, rows, :], m_scr[2, rows, :]
        mm = jnp.maximum(jnp.maximum(m0, m1), m2)
        e0, e1, e2 = jnp.exp(m0 - mm), jnp.exp(m1 - mm), jnp.exp(m2 - mm)
        num = e0 * acc_scr[0, rows, :] + e1 * acc_scr[1, rows, :] + e2 * acc_scr[2, rows, :]
        den = e0 * l_scr[0, rows, :] + e1 * l_scr[1, rows, :] + e2 * l_scr[2, rows, :]
        o_ref[rows, :] = (num / den).astype(o_ref.dtype)
        return carry

    lax.fori_loop(0, tb // rc, combine, 0)


def _attn_a(a1, a4, a16, bias):
    B, _, S, _ = a1.shape
    tb = TB_A
    npair = HEADS_A // 2

    def qspec(d):
        return pl.BlockSpec((None, d, tb // d, LANES), lambda b, p, t: (b, 0, t, p))

    def kvspec(d, off):
        return pl.BlockSpec((None, d, S // d, LANES), lambda b, p, t, off=off: (b, 0, 0, off + p))

    in_specs = [qspec(1), qspec(4), qspec(16)]
    args = [a1, a4, a16]
    for d, arr in ((1, a1), (4, a4), (16, a16)):
        in_specs += [kvspec(d, npair), kvspec(d, 2 * npair)]
        args += [arr, arr]
    in_specs.append(pl.BlockSpec((3, 3, 2, TQ, TQ + 2 * HALF_A), lambda b, p, t: (0, 0, p, 0, 0)))
    args.append(bias)
    return pl.pallas_call(
        functools.partial(_attn_a_kernel, seq=S),
        grid=(B, npair, S // tb),
        in_specs=in_specs,
        out_specs=pl.BlockSpec((None, tb, LANES), lambda b, p, t: (b, t, p)),
        out_shape=jax.ShapeDtypeStruct((B, S, WA), jnp.bfloat16),
        scratch_shapes=[pltpu.VMEM((3, tb, LANES), jnp.float32)] * 3,
        compiler_params=pltpu.CompilerParams(
            dimension_semantics=("parallel", "parallel", "arbitrary"), vmem_limit_bytes=VMEM_LIMIT),
        name="attn_a",
    )(*args)


def _attn_b_kernel(q_ref, k_ref, v_ref, bias_ref, sink_ref, o_ref, *, seq):
    tb = o_ref.shape[0]
    tbi = pl.program_id(2)
    lo = lax.broadcasted_iota(jnp.int32, (TQ, LANES), 1) < HEAD_DIM
    W = TQ + 2 * HALF_B
    s0 = sink_ref[0, 0:1, 0:1]
    s1 = sink_ref[1, 0:1, 0:1]

    def tiles(it, carry):
        work = []
        for u in range(TILES_B):
            jj = it * TILES_B + u
            sub0 = tbi * tb + jj * TQ
            ws = pl.multiple_of(jnp.clip(sub0 - HALF_B, 0, seq - W), HALF_B)
            var = (sub0 - ws) // HALF_B
            rows = pl.ds(pl.multiple_of(jj * TQ, TQ), TQ)
            out = []
            work.append((_band_tile(q_ref[rows, :], k_ref[pl.ds(ws, W), :], v_ref[pl.ds(ws, W), :],
                                    bias_ref[var], lo, out, sinks=(s0, s1)), out, rows))
        _round_robin([w[0] for w in work])
        for _, ((m0, r0), (m1, r1)), rows in work:
            num = jnp.where(lo, r0[:, :LANES], r1[:, :LANES])
            den = jnp.where(lo, r0[:, LANES:] + jnp.exp(s0 - m0), r1[:, LANES:] + jnp.exp(s1 - m1))
            o_ref[rows, :] = (num / den).astype(o_ref.dtype)
        return carry

    lax.fori_loop(0, tb // TQ // TILES_B, tiles, 0)


def _attn_b(qb, kb, vb, bias, sink):
    B, S, _ = qb.shape
    tb = TB_B
    npair = HEADS_B // 2
    return pl.pallas_call(
        functools.partial(_attn_b_kernel, seq=S),
        grid=(B, npair, S // tb),
        in_specs=[
            pl.BlockSpec((None, tb, LANES), lambda b, p, t: (b, t, p)),
            pl.BlockSpec((None, S, LANES), lambda b, p, t: (b, 0, 0)),
            pl.BlockSpec((None, S, LANES), lambda b, p, t: (b, 0, 0)),
            pl.BlockSpec((3, None, 2, TQ, TQ + 2 * HALF_B), lambda b, p, t: (0, p, 0, 0, 0)),
            pl.BlockSpec((None, 2, 8, LANES), lambda b, p, t: (p, 0, 0, 0)),
        ],
        out_specs=pl.BlockSpec((None, tb, LANES), lambda b, p, t: (b, t, p)),
        out_shape=jax.ShapeDtypeStruct((B, S, WBQ), jnp.bfloat16),
        compiler_params=pltpu.CompilerParams(
            dimension_semantics=("parallel", "parallel", "arbitrary"), vmem_limit_bytes=VMEM_LIMIT),
        name="attn_b",
    )(qb, kb, vb, bias, sink)


def _split3(x):
    bf = jnp.bfloat16
    p1 = x.astype(bf).astype(jnp.float32)
    r1 = x - p1
    p2 = r1.astype(bf).astype(jnp.float32)
    return p1, p2, (r1 - p2).astype(bf).astype(jnp.float32)


def _mlstm_chain(q2, k2, vext, gt, lf, bcs, ics, tri, state, mrow, lo, out):
    f32 = jnp.float32
    bf = jnp.bfloat16
    b_row = sum(lax.dot_general(p.astype(bf), tri, _NT, preferred_element_type=f32) for p in _split3(lf))
    zero = jnp.zeros_like(q2)
    qks = [lax.dot_general(jnp.where(lo if hh == 0 else jnp.logical_not(lo), q2, zero), k2, _NT,
                           preferred_element_type=f32) for hh in (0, 1)]
    inter = jnp.dot(q2, state.astype(bf), preferred_element_type=f32)
    yield
    b_tot = jnp.sum(lf, axis=1, keepdims=True)
    scal = []
    for hh in (0, 1):
        br = b_row[2 + hh:3 + hh, :]
        ir = gt[hh:hh + 1, :]
        bt = b_tot[2 + hh:3 + hh, :]
        mp = mrow[hh:hh + 1, :]
        m_loc = jnp.max(bt - br + ir, axis=1, keepdims=True)
        m_new = jnp.maximum(bt + mp, m_loc)
        scal.append((br, ir, mp, bt - m_new, jnp.exp(bt + mp - m_new), m_new))
    visible = tri > 0
    heads = []
    for hh in (0, 1):
        br, ir, mp = scal[hh][:3]
        bc = bcs[hh]
        dmat = jnp.where(visible, bc - br + ir, NEG)
        m_in = bc + mp
        m_t = jnp.maximum(jnp.max(dmat, axis=1, keepdims=True), m_in)
        p = (jnp.exp(dmat - m_t) * qks[hh]).astype(bf)
        r = jnp.dot(p, vext, preferred_element_type=f32)
        yield
        heads.append((r, jnp.exp(m_in - m_t), m_t))
    (r0, a0, mt0), (r1, a1, mt1) = heads
    e2 = jnp.exp(jnp.where(lo, ics[0] - bcs[0] + scal[0][3], ics[1] - bcs[1] + scal[1][3]))
    ke = (k2.astype(f32) * e2).astype(bf)
    s_loc = lax.dot_general(ke, vext, _TN, preferred_element_type=f32)
    yield

    a2 = jnp.where(lo, a0, a1)
    mt2 = jnp.where(lo, mt0, mt1)
    num = jnp.where(lo, r0[:, :LANES], r1[:, :LANES]) + a2 * inter[:, :LANES]
    den = jnp.where(lo, r0[:, LANES:], r1[:, LANES:]) + a2 * inter[:, LANES:]
    h = num / jnp.maximum(jnp.abs(den), jnp.exp(-mt2))

    row = lax.broadcasted_iota(jnp.int32, (LANES, 2 * LANES), 0)
    col = lax.broadcasted_iota(jnp.int32, (LANES, 2 * LANES), 1)
    same_head = (row < HEAD_DIM) == ((col % LANES) < HEAD_DIM)
    dec_rows = jnp.where(row < HEAD_DIM, scal[0][4][:, 0:1], scal[1][4][:, 0:1])
    new_state = dec_rows * state + jnp.where(same_head, s_loc, 0.0)
    out.append((h, new_state, scal[0][5], scal[1][5]))


def _mlstm_kernel(fq, fk, fv, bq, bk, bv, gtf_ref, gtb_ref, gcf_ref, gcb_ref, gbias_ref, gbcol_ref, tri_ref,
                  hf_ref, hb_ref, st_scr, m_scr):
    c = pl.program_id(1)
    f32 = jnp.float32
    bf = jnp.bfloat16
    lo = lax.broadcasted_iota(jnp.int32, (LC, LANES), 1) < HEAD_DIM

    @pl.when(c == 0)
    def _():
        st_scr[...] = jnp.zeros_like(st_scr)
        m_scr[...] = jnp.zeros_like(m_scr)

    dirs = ((fq, fk, fv, gtf_ref, gcf_ref, hf_ref), (bq, bk, bv, gtb_ref, gcb_ref, hb_ref))
    chains = []
    for bi in range(fq.shape[0]):
        for drn, (q_ref, k_ref, v_ref, gt_ref, gc_ref, h_ref) in enumerate(dirs):
            tri = tri_ref[drn]
            gt_all = (gt_ref[bi] + gbias_ref[:, drn]).reshape(3 * 8, LC)
            lf_all = jnp.minimum(gt_all, 0.0) - jnp.log(1.0 + jnp.exp(-jnp.abs(gt_all)))
            bcol = sum(lax.dot_general(tri, p.astype(bf), _NT, preferred_element_type=f32) for p in _split3(lf_all))
            gcol = gc_ref[bi] + gbcol_ref[...]
            for pr in range(HEADS_C // 2):
                sl = slice(pr * LANES, (pr + 1) * LANES)
                v2 = v_ref[bi, :, sl]
                vext = jnp.concatenate([v2, jnp.ones_like(v2)], axis=1)
                base = (pr * 2 + drn) * 8
                ics = [jnp.broadcast_to(gcol[:, base + hh:base + hh + 1], (LC, LANES)) for hh in (0, 1)]
                bcs = [jnp.broadcast_to(bcol[:, pr * 8 + 2 + hh:pr * 8 + 3 + hh], (LC, LANES)) for hh in (0, 1)]
                out = []
                gen = _mlstm_chain(q_ref[bi, :, sl], k_ref[bi, :, sl], vext, gt_all[pr * 8:(pr + 1) * 8],
                                   lf_all[pr * 8:(pr + 1) * 8], bcs, ics, tri,
                                   st_scr[bi, drn, pr], m_scr[bi, drn, pr], lo, out)
                chains.append((gen, out, (bi, drn, pr), sl, h_ref))

    _round_robin([ch[0] for ch in chains])
    for _, out, (bi, drn, pr), sl, h_ref in chains:
        h, new_state, mn0, mn1 = out[0]
        st_scr[bi, drn, pr] = new_state
        m_scr[bi, drn, pr, 0:1, :] = mn0
        m_scr[bi, drn, pr, 1:2, :] = mn1
        h_ref[bi, :, sl] = h.astype(h_ref.dtype)


def _mlstm(qc, kc, vc, gt, gcol, gate_b, gate_bcol, tri):
    B, S, _ = qc.shape
    nc = S // LC
    npair = HEADS_C // 2

    def pos(drn, c):
        return c if drn == 0 else nc - 1 - c

    nb = MLSTM_BATCH

    def cur(drn):
        return pl.BlockSpec((nb, LC, WC), lambda b, c: (b, pos(drn, c), 0))

    def gates(drn):
        return pl.BlockSpec((nb, npair, None, 8, LC), lambda b, c: (b, 0, drn, 0, pos(drn, c)))

    def const(shape):
        return pl.BlockSpec(shape, lambda b, c: (0,) * len(shape))

    in_specs, args = [], []
    for drn in (0, 1):
        in_specs += [cur(drn), cur(drn), cur(drn)]
        args += [qc, kc, vc]
    in_specs += [gates(0), gates(1), pl.BlockSpec((nb, LC, LANES), lambda b, c: (b, pos(0, c), 0)),
                 pl.BlockSpec((nb, LC, LANES), lambda b, c: (b, pos(1, c), 0)),
                 const(gate_b.shape), const(gate_bcol.shape), const(tri.shape)]
    args += [gt, gt, gcol, gcol, gate_b, gate_bcol, tri]
    h_shape = jax.ShapeDtypeStruct((B, S, WC), jnp.bfloat16)
    return pl.pallas_call(
        _mlstm_kernel,
        grid=(B // nb, nc),
        in_specs=in_specs,
        out_specs=(cur(0), cur(1)),
        out_shape=(h_shape, h_shape),
        scratch_shapes=[
            pltpu.VMEM((nb, 2, npair, LANES, 2 * LANES), jnp.float32),
            pltpu.VMEM((nb, 2, npair, 8, LANES), jnp.float32),
        ],
        compiler_params=pltpu.CompilerParams(
            dimension_semantics=("parallel", "arbitrary"), vmem_limit_bytes=VMEM_LIMIT),
        name="mlstm",
    )(*args)


def _post_kernel(x_ref, ya_ref, yb_ref, hf_ref, hb_ref, oc_ref, p_ref, ng_ref, hm_ref, wo_ref, fg_ref,
                 wup_ref, wdn_ref, pp_ref, pg_ref, fin_ref, out_ref, *, final):
    f32 = jnp.float32
    bf = jnp.bfloat16
    hs = hf_ref[...].astype(f32) + hb_ref[...].astype(f32)
    sq = hs * hs
    sq_hi = sq.astype(bf)
    sq_lo = (sq - sq_hi.astype(f32)).astype(bf)
    ms = (jnp.dot(sq_hi, hm_ref[...], preferred_element_type=f32)
          + jnp.dot(sq_lo, hm_ref[...], preferred_element_type=f32))
    yc = (_sigmoid(oc_ref[...].astype(f32)) * (hs * lax.rsqrt(ms + EPS) * ng_ref[...])).astype(bf)
    y = jnp.concatenate([ya_ref[...], yb_ref[...], yc], axis=1)
    x1 = x_ref[...] + jnp.dot(y, wo_ref[...], preferred_element_type=f32)
    emb = jnp.dot(p_ref[...].astype(bf), pp_ref[...], preferred_element_type=f32)
    h = _rms(x1, fg_ref[...]).astype(bf)
    ffn = None
    for j in range(D_FF // FF_CHUNK):
        gate = jnp.dot(h, wup_ref[:, j * FF_CHUNK:(j + 1) * FF_CHUNK], preferred_element_type=f32)
        up = jnp.dot(h, wup_ref[:, D_FF + j * FF_CHUNK:D_FF + (j + 1) * FF_CHUNK], preferred_element_type=f32)
        act = (gate * _sigmoid(gate) * up).astype(bf)
        part = jnp.dot(act, wdn_ref[j * FF_CHUNK:(j + 1) * FF_CHUNK, :], preferred_element_type=f32)
        ffn = part if ffn is None else ffn + part
    x2 = x1 + ffn
    gate = _sigmoid(jnp.dot(x2.astype(bf), pg_ref[...], preferred_element_type=f32))
    x3 = x2 + emb * gate
    if final:
        x3 = _rms(x3, fin_ref[...])
    out_ref[...] = x3


def _post(x, ya, yb, hf, hb, oc, p, layer, ng, hm, wo, fg, wup, wdn, pp, pg, fin, final):
    B, S, D = x.shape
    tm = TM_POST

    def tok(width):
        return pl.BlockSpec((None, tm, width), lambda b, i: (b, i, 0))

    def const(shape):
        return pl.BlockSpec(shape, lambda b, i: (0,) * len(shape), pipeline_mode=pl.Buffered(1))

    return pl.pallas_call(
        functools.partial(_post_kernel, final=final),
        grid=(B, S // tm),
        in_specs=[
            tok(D), tok(WA), tok(WBQ), tok(WC), tok(WC), tok(WC),
            pl.BlockSpec((None, None, tm, p.shape[-1]), lambda b, i: (layer, b, i, 0)),
            const(ng.shape), const(hm.shape), const(wo.shape), const((1, D)),
            const(wup.shape), const(wdn.shape), const(pp.shape), const(pg.shape), const((1, D)),
        ],
        out_specs=tok(D),
        out_shape=jax.ShapeDtypeStruct((B, S, D), jnp.float32),
        compiler_params=pltpu.CompilerParams(
            dimension_semantics=("parallel", "parallel"), vmem_limit_bytes=VMEM_LIMIT),
        name="post",
    )(x, ya, yb, hf, hb, oc, p, ng, hm, wo, fg, wup, wdn, pp, pg, fin)


def _t5_bucket(rel):
    half_b = N_BUCKETS // 2
    max_exact = half_b // 2
    n = jnp.abs(rel)
    nf = jnp.maximum(n, 1).astype(jnp.float32)
    large = max_exact + (jnp.log(nf / max_exact) / jnp.log(jnp.float32(REL_MAX_DIST / max_exact))
                         * (half_b - max_exact)).astype(jnp.int32)
    large = jnp.minimum(large, half_b - 1)
    return jnp.where(rel > 0, half_b, 0) + jnp.where(n < max_exact, n, large)


def _band_bias(table, half, dil):
    W = TQ + 2 * half
    H = table.shape[1]
    L = W + TQ
    j = jnp.arange(L, dtype=jnp.int32)
    off = jnp.where(j < W, j, j - L)
    tiles = []
    for shift in (0, half, 2 * half):
        rel = off - shift
        vals = jnp.where((jnp.abs(rel) <= half)[:, None], table[_t5_bucket(rel * dil)].astype(jnp.float32), NEG)
        row = jnp.broadcast_to(vals.T[:, None, :], (H, TQ, L)).reshape(H, TQ * L)
        tiles.append(row[:, :TQ * (L - 1)].reshape(H, TQ, L - 1)[:, :, :W])
    return jnp.stack(tiles, 0)


_B_ORDER = (0, 2, 1, 3)


def _perm_heads(w, order, axis):
    blocks = [lax.slice_in_dim(w, h * HEAD_DIM, (h + 1) * HEAD_DIM, axis=axis) for h in order]
    return jnp.concatenate(blocks, axis=axis)


def kernel(x, p, rel_bias, attn_norm, w_in, qk_conv, gate_bias, sink_logits, mlstm_norm, w_out, ffn_norm,
           w_up, w_down, ple_proj, ple_gate, final_norm):
    B, S, D = x.shape
    depth = w_in.shape[0]
    assert D == D_MODEL and S % TB_A == 0 and S // 16 >= TQ + 2 * HALF_A and B % MLSTM_BATCH == 0
    f32 = jnp.float32
    bf = jnp.bfloat16

    bias_a = jnp.stack([_band_bias(rel_bias[:, :HEADS_A], HALF_A, d) for d in DILATIONS], 0)
    bias_b = _band_bias(rel_bias[:, HEADS_A:], HALF_B, 1)
    bias_b = jnp.stack([bias_b[:, h] for h in _B_ORDER], 1).reshape(3, 2, 2, TQ, TQ + 2 * HALF_B)
    tri_lo = (jnp.arange(LC)[:, None] >= jnp.arange(LC)[None, :]).astype(f32)
    tri = jnp.stack([tri_lo, tri_lo.T], 0).astype(bf)
    head_id = jnp.arange(WC) // HEAD_DIM
    head_mean = ((head_id[:, None] == head_id[None, :]).astype(f32) / HEAD_DIM).astype(bf)

    gate_off = D_MAIN
    for i in range(depth):
        w = w_in[i]
        qb_off = 3 * WA
        w_main = jnp.concatenate([
            w[:, :qb_off], _perm_heads(w[:, qb_off:qb_off + WBQ], _B_ORDER, 1), w[:, qb_off + WBQ:gate_off]],
            axis=1).astype(bf)
        wg = w[:, gate_off:]
        gb = gate_bias[i]
        rows, brow = [], []
        zrow = jnp.zeros((4, D), f32)
        for pr in range(HEADS_C // 2):
            for drn in range(2):
                idx = [(2 * drn) * HEADS_C + 2 * pr, (2 * drn) * HEADS_C + 2 * pr + 1,
                       (2 * drn + 1) * HEADS_C + 2 * pr, (2 * drn + 1) * HEADS_C + 2 * pr + 1]
                rows += [wg[:, idx].T, zrow]
                brow += [gb[jnp.array(idx)], jnp.zeros((4,), f32)]
        wg_t = jnp.concatenate(rows, 0).astype(bf)
        wg_c = jnp.concatenate([wg_t.T, jnp.zeros((D, LANES - 48), bf)], axis=1)
        gate_b = jnp.broadcast_to(jnp.concatenate(brow).reshape(3, 2, 8, 1), (3, 2, 8, LC))
        gate_bcol = jnp.concatenate([jnp.concatenate(brow), jnp.zeros((LANES - 48,), f32)])[None]
        conv = jnp.concatenate([qk_conv[i], jnp.zeros((8 - CONV_WIDTH, 2 * WC), f32)], 0)
        sink = jnp.broadcast_to(sink_logits[i][jnp.array(_B_ORDER)].reshape(2, 2, 1, 1), (2, 2, 8, LANES))
        wo = w_out[i]
        wo = jnp.concatenate([wo[:WA], _perm_heads(wo[WA:WA + WBQ], _B_ORDER, 0), wo[WA + WBQ:]], axis=0).astype(bf)

        a1, a4, a16, qb, kb, vb, qc, kc, vc, oc, gt, gcol = _inproj(x, attn_norm[i][None], w_main, wg_t, wg_c, conv)
        ya = _attn_a(a1.reshape(B, 1, S, 3 * WA), a4, a16, bias_a)
        yb = _attn_b(qb, kb, vb, bias_b, sink)
        hf, hb = _mlstm(qc, kc, vc, gt, gcol, gate_b, gate_bcol, tri)
        x = _post(x, ya, yb, hf, hb, oc, p, i, mlstm_norm[i][None], head_mean, wo, ffn_norm[i][None], w_up[i].astype(bf), w_down[i].astype(bf),
                  ple_proj[i].astype(bf), ple_gate[i].astype(bf), final_norm[None], final=(i == depth - 1))
    return x
```
